```python
import math
import jax, jax.numpy as jnp
from jax import lax
import numpy as np

D_MODEL = 2048
BATCH = 16
SEQ = 256
DEPTH = 2
DEC_BATCH = 2
DEC_SEQ = 2048
PAST_LEN = 512

GRID_W = 64
N_DIRS = 2
S5_WIDTH = 1024
S5_GROUP = 16
S5_GROUPS = S5_WIDTH // S5_GROUP
S5_STATE = 64
GLA_HEADS = 4
GLA_DK = 128
GLA_DV = 256
GLA_QK = GLA_HEADS * GLA_DK
GLA_V = GLA_HEADS * GLA_DV
GLA_RANK = 16
GLA_GATE_NORM = 16.0
GLA_CHUNK = 16
IN_SPLITS = (S5_WIDTH, GLA_QK, GLA_QK, GLA_V, GLA_V, N_DIRS * GLA_RANK, D_MODEL, D_MODEL)
IN_COLS = sum(IN_SPLITS)
N_EXPERTS = 32
TOP_K = 4
D_EXPERT = D_MODEL
SWIGLU_LIMIT = 7.0
SWIGLU_ALPHA = 1.702
MOE_BLOCK = 128
POS_BASE = 10000.0
EPS = 1e-6

kernel_name = 'hybrid_s5_gla_moe_diffusion_step'


def rms_norm(x, g):
    xf = x.astype(jnp.float32)
    y = xf * lax.rsqrt(jnp.mean(xf * xf, axis=-1, keepdims=True) + EPS)
    return (y * g.astype(jnp.float32)).astype(x.dtype)


def grid_pos_embed(n_tokens, dtype):
    rows = n_tokens // GRID_W
    rr, cc = jnp.meshgrid(jnp.arange(rows, dtype=jnp.float32),
                          jnp.arange(GRID_W, dtype=jnp.float32), indexing='ij')
    quarter = D_MODEL // 4
    omega = 1.0 / (POS_BASE ** (jnp.arange(quarter, dtype=jnp.float32) / quarter))

    def emb(pos):
        ang = pos[:, None] * omega[None, :]
        return jnp.concatenate([jnp.sin(ang), jnp.cos(ang)], axis=-1)

    pe = jnp.concatenate([emb(rr.reshape(-1)), emb(cc.reshape(-1))], axis=-1)
    return pe.astype(dtype)


def s5_discretize(lam_re, lam_im, log_dt, b_re, b_im):
    dt = jnp.exp(log_dt)[:, None]
    z_re, z_im = lam_re * dt, lam_im * dt
    mag = jnp.exp(z_re)
    a_re, a_im = mag * jnp.cos(z_im), mag * jnp.sin(z_im)
    den = lam_re * lam_re + lam_im * lam_im
    n_re, n_im = a_re - 1.0, a_im
    c_re = (n_re * lam_re + n_im * lam_im) / den
    c_im = (n_im * lam_re - n_re * lam_im) / den
    bb_re = c_re[..., None] * b_re - c_im[..., None] * b_im
    bb_im = c_re[..., None] * b_im + c_im[..., None] * b_re
    return a_re, a_im, bb_re, bb_im


def _complex_affine_combine(e1, e2):
    a1r, a1i, b1r, b1i = e1
    a2r, a2i, b2r, b2i = e2
    return (a2r * a1r - a2i * a1i, a2r * a1i + a2i * a1r,
            a2r * b1r - a2i * b1i + b2r, a2r * b1i + a2i * b1r + b2i)


def s5_direction(u, h0_re, h0_im, lam_re, lam_im, log_dt, b_re, b_im, c_re, c_im):
    a_re, a_im, bb_re, bb_im = s5_discretize(lam_re, lam_im, log_dt, b_re, b_im)
    bu_re = jnp.einsum('blgh,gph->blgp', u, bb_re)
    bu_im = jnp.einsum('blgh,gph->blgp', u, bb_im)
    bu_re = bu_re.at[:, 0].add(a_re * h0_re - a_im * h0_im)
    bu_im = bu_im.at[:, 0].add(a_re * h0_im + a_im * h0_re)
    ar = jnp.broadcast_to(a_re, bu_re.shape)
    ai = jnp.broadcast_to(a_im, bu_re.shape)
    _, _, h_re, h_im = lax.associative_scan(_complex_affine_combine, (ar, ai, bu_re, bu_im), axis=1)
    y = jnp.einsum('blgp,ghp->blgh', h_re, c_re) - jnp.einsum('blgp,ghp->blgh', h_im, c_im)
    return y, h_re[:, -1], h_im[:, -1]


def s5_mixer(u, h0_re, h0_im, p):
    bsz, L, _ = u.shape
    f = lambda t: t.astype(jnp.float32)
    uf = f(u).reshape(bsz, L, S5_GROUPS, S5_GROUP)
    y_f, hf_re, hf_im = s5_direction(
        uf, f(h0_re[:, 0]), f(h0_im[:, 0]), f(p['s5_lam_re'][0]), f(p['s5_lam_im'][0]),
        f(p['s5_log_dt'][0]), f(p['s5_b_re'][0]), f(p['s5_b_im'][0]),
        f(p['s5_c_re'][0]), f(p['s5_c_im'][0]))
    y_b, hb_re, hb_im = s5_direction(
        uf[:, ::-1], f(h0_re[:, 1]), f(h0_im[:, 1]), f(p['s5_lam_re'][1]), f(p['s5_lam_im'][1]),
        f(p['s5_log_dt'][1]), f(p['s5_b_re'][1]), f(p['s5_b_im'][1]),
        f(p['s5_c_re'][1]), f(p['s5_c_im'][1]))
    y = y_f + y_b[:, ::-1] + f(p['s5_d']) * uf
    y = y.reshape(bsz, L, S5_WIDTH).astype(u.dtype)
    g = jax.nn.gelu(y)
    out = g * jax.nn.sigmoid(g @ p['s5_w_glu'] + p['s5_b_glu'])
    return out, jnp.stack([hf_re, hb_re], axis=1), jnp.stack([hf_im, hb_im], axis=1)


def gla_direction(q, k, v, logg, s0):
    bsz, L = q.shape[:2]
    n = L // GLA_CHUNK
    chunks = lambda t: t.reshape(bsz, n, GLA_CHUNK, *t.shape[2:])
    q, k, v, logg = chunks(q), chunks(k), chunks(v), chunks(logg)
    b = jnp.cumsum(logg, axis=2)
    b_last = b[:, :, -1]
    causal = jnp.tril(jnp.ones((GLA_CHUNK, GLA_CHUNK), dtype=bool))
    diff = b[:, :, :, None] - b[:, :, None, :]
    decay = jnp.exp(jnp.where(causal[:, :, None, None], diff, -jnp.inf))
    scores = jnp.einsum('bnthd,bnshd,bntshd->bnhts', q, k, decay)
    o_intra = jnp.einsum('bnhts,bnshv->bnthv', scores, v)
    q_in = q * jnp.exp(b)
    k_st = k * jnp.exp(b_last[:, :, None] - b)
    d_state = jnp.einsum('bnchd,bnchv->bnhdv', k_st, v)
    a_chunk = jnp.exp(b_last)

    def step(s, xs):
        qc, ac, dsc = xs
        o = jnp.einsum('bchd,bhdv->bchv', qc, s)
        return ac[..., None] * s + dsc, o

    s_fin, o_inter = lax.scan(step, s0, (jnp.moveaxis(q_in, 1, 0), jnp.moveaxis(a_chunk, 1, 0),
                                         jnp.moveaxis(d_state, 1, 0)))
    o = o_intra + jnp.moveaxis(o_inter, 0, 1)
    return o.reshape(bsz, L, GLA_HEADS, GLA_DV), s_fin


def gla_mixer(q, k, v, r, glr, s0, p):
    bsz, L, _ = q.shape
    f = lambda t: t.astype(jnp.float32)
    qf = f(q).reshape(bsz, L, GLA_HEADS, GLA_DK) * (GLA_DK ** -0.5)
    kf = f(k).reshape(bsz, L, GLA_HEADS, GLA_DK)
    vf = f(v).reshape(bsz, L, GLA_HEADS, GLA_DV)
    glr = f(glr).reshape(bsz, L, N_DIRS, GLA_RANK)
    gate = jnp.einsum('bldr,drk->bldk', glr, f(p['gla_w_g2'])) + f(p['gla_b_g'])
    logg = (jax.nn.log_sigmoid(gate) / GLA_GATE_NORM).reshape(bsz, L, N_DIRS, GLA_HEADS, GLA_DK)
    s0 = f(s0)
    o_f, s_f = gla_direction(qf, kf, vf, logg[:, :, 0], s0[:, 0])
    flip = lambda t: t[:, ::-1]
    o_b, s_b = gla_direction(flip(qf), flip(kf), flip(vf), flip(logg[:, :, 1]), s0[:, 1])
    o = o_f + flip(o_b)
    o = o * lax.rsqrt(jnp.mean(o * o, axis=-1, keepdims=True) + EPS) * f(p['gla_norm'])
    o = o.reshape(bsz, L, GLA_V).astype(r.dtype) * jax.nn.silu(r)
    return o, jnp.stack([s_f, s_b], axis=1)


def clamped_swiglu(h):
    h_glu, h_lin = h[..., ::2], h[..., 1::2]
    h_glu = jnp.minimum(h_glu, SWIGLU_LIMIT)
    h_lin = jnp.clip(h_lin, -SWIGLU_LIMIT, SWIGLU_LIMIT)
    return h_glu * jax.nn.sigmoid(SWIGLU_ALPHA * h_glu) * (h_lin + 1.0)


def moe(x, p):
    bsz, L, d = x.shape
    xt = x.reshape(-1, d)
    T = xt.shape[0]
    logits = xt.astype(jnp.float32) @ p['w_router'].astype(jnp.float32) + p['b_router'].astype(jnp.float32)
    top_val, top_idx = lax.top_k(logits, TOP_K)
    gates = jax.nn.softmax(top_val, axis=-1)
    M = T * TOP_K
    e_flat = top_idx.reshape(-1)
    tok_flat = jnp.arange(M, dtype=jnp.int32) // TOP_K
    order = jnp.argsort(e_flat)
    e_sorted, tok_sorted, g_sorted = e_flat[order], tok_flat[order], gates.reshape(-1)[order]
    counts = jnp.bincount(e_flat, length=N_EXPERTS)
    starts = jnp.cumsum(counts) - counts
    padded = (counts + MOE_BLOCK - 1) // MOE_BLOCK * MOE_BLOCK
    pad_end = jnp.cumsum(padded)
    pad_start = pad_end - padded
    dest = pad_start[e_sorted] + (jnp.arange(M) - starts[e_sorted])
    n_blocks = (M + N_EXPERTS * MOE_BLOCK + MOE_BLOCK - 1) // MOE_BLOCK
    n_rows = n_blocks * MOE_BLOCK
    row_tok = jnp.zeros((n_rows,), jnp.int32).at[dest].set(tok_sorted)
    row_gate = jnp.zeros((n_rows,), jnp.float32).at[dest].set(g_sorted)
    block_exp = jnp.minimum(jnp.searchsorted(pad_end, jnp.arange(n_blocks) * MOE_BLOCK, side='right'),
                            N_EXPERTS - 1)
    w1, b1, w2, b2 = p['w_e1'], p['b_e1'], p['w_e2'], p['b_e2']

    def run_block(args):
        tok, e = args
        h = xt[tok] @ w1[e] + b1[e]
        return clamped_swiglu(h) @ w2[e] + b2[e]

    y = lax.map(run_block, (row_tok.reshape(n_blocks, MOE_BLOCK), block_exp))
    y = y.reshape(n_rows, d).astype(jnp.float32) * row_gate[:, None]
    out = jnp.zeros((T, d), jnp.float32).at[row_tok].add(y)
    return out.reshape(bsz, L, d).astype(x.dtype)


def layer_forward(x, mod, s5_h0_re, s5_h0_im, gla_s0, p):
    sh1, sc1, g1, sh2, sc2, g2 = jnp.split(mod, 6, axis=-1)
    h = rms_norm(x, p['norm1']) * (1.0 + sc1) + sh1
    proj = h @ p['w_in']
    offsets = np.cumsum(IN_SPLITS)[:-1].tolist()
    u, q, k, v, r, glr, ga, gb = jnp.split(proj, offsets, axis=-1)
    ya, s5_re, s5_im = s5_mixer(u, s5_h0_re, s5_h0_im, p)
    yb, gla_s = gla_mixer(q, k, v, r, glr, gla_s0, p)
    merged = jax.nn.sigmoid(ga) * (ya @ p['w_s5_out']) + jax.nn.sigmoid(gb) * (yb @ p['w_gla_out'])
    x = x + g1 * (merged @ p['w_out'])
    h2 = rms_norm(x, p['norm2']) * (1.0 + sc2) + sh2
    x = x + g2 * moe(h2, p)
    return x, s5_re, s5_im, gla_s


def setup_inputs(seed: int = 0) -> dict:
    key = jax.random.key(seed)
    ks = iter(jax.random.split(key, 40))
    nrm = lambda shape, s: jax.random.normal(next(ks), shape, jnp.float32) * s
    G, P, H = S5_GROUPS, S5_STATE, S5_GROUP
    lam_im_base = jnp.pi * jnp.arange(P, dtype=jnp.float32)
    return {
        'x_prompt': nrm((BATCH, SEQ, D_MODEL), 1.0),
        'x_sample': nrm((DEC_BATCH, DEC_SEQ, D_MODEL), 1.0),
        'c': nrm((DEC_BATCH, D_MODEL), 1.0),
        'state_s5_re': nrm((DEC_BATCH, DEPTH, N_DIRS, G, P), 0.3),
        'state_s5_im': nrm((DEC_BATCH, DEPTH, N_DIRS, G, P), 0.3),
        'state_gla': nrm((DEC_BATCH, DEPTH, N_DIRS, GLA_HEADS, GLA_DK, GLA_DV), 1.0),
        'c_ctx': nrm((D_MODEL,), 1.0),
        'w_mod': nrm((DEPTH, D_MODEL, 6 * D_MODEL), D_MODEL ** -0.5),
        'b_mod': nrm((DEPTH, 6 * D_MODEL), 0.02),
        'norm1': 1.0 + nrm((DEPTH, D_MODEL), 0.02),
        'w_in': nrm((DEPTH, D_MODEL, IN_COLS), D_MODEL ** -0.5),
        's5_lam_re': -0.5 + nrm((DEPTH, N_DIRS, G, P), 0.01),
        's5_lam_im': lam_im_base + nrm((DEPTH, N_DIRS, G, P), 0.01),
        's5_log_dt': jax.random.uniform(next(ks), (DEPTH, N_DIRS, G), jnp.float32,
                                        math.log(1e-3), math.log(1e-1)),
        's5_b_re': nrm((DEPTH, N_DIRS, G, P, H), (2 * H) ** -0.5),
        's5_b_im': nrm((DEPTH, N_DIRS, G, P, H), (2 * H) ** -0.5),
        's5_c_re': nrm((DEPTH, N_DIRS, G, H, P), (2 * P) ** -0.5),
        's5_c_im': nrm((DEPTH, N_DIRS, G, H, P), (2 * P) ** -0.5),
        's5_d': nrm((DEPTH, G, H), 1.0),
        's5_w_glu': nrm((DEPTH, S5_WIDTH, S5_WIDTH), S5_WIDTH ** -0.5),
        's5_b_glu': nrm((DEPTH, S5_WIDTH), 0.02),
        'w_s5_out': nrm((DEPTH, S5_WIDTH, D_MODEL), S5_WIDTH ** -0.5),
        'gla_w_g2': nrm((DEPTH, N_DIRS, GLA_RANK, GLA_QK), GLA_RANK ** -0.5),
        'gla_b_g': nrm((DEPTH, N_DIRS, GLA_QK), 0.1),
        'gla_norm': 1.0 + nrm((DEPTH, GLA_DV), 0.02),
        'w_gla_out': nrm((DEPTH, GLA_V, D_MODEL), GLA_V ** -0.5),
        'w_out': nrm((DEPTH, D_MODEL, D_MODEL), D_MODEL ** -0.5),
        'norm2': 1.0 + nrm((DEPTH, D_MODEL), 0.02),
        'w_router': nrm((DEPTH, D_MODEL, N_EXPERTS), D_MODEL ** -0.5),
        'b_router': nrm((DEPTH, N_EXPERTS), 0.01),
        'w_e1': nrm((DEPTH, N_EXPERTS, D_MODEL, 2 * D_EXPERT), D_MODEL ** -0.5),
        'b_e1': nrm((DEPTH, N_EXPERTS, 2 * D_EXPERT), 0.02),
        'w_e2': nrm((DEPTH, N_EXPERTS, D_EXPERT, D_MODEL), D_EXPERT ** -0.5),
        'b_e2': nrm((DEPTH, N_EXPERTS, D_MODEL), 0.02),
        'norm_f': 1.0 + nrm((D_MODEL,), 0.02),
    }


def reference(x_prompt, x_sample, c, state_s5_re, state_s5_im, state_gla, c_ctx,
              w_mod, b_mod, norm1, w_in, s5_lam_re, s5_lam_im, s5_log_dt, s5_b_re, s5_b_im,
              s5_c_re, s5_c_im, s5_d, s5_w_glu, s5_b_glu, w_s5_out, gla_w_g2, gla_b_g, gla_norm,
              w_gla_out, w_out, norm2, w_router, b_router, w_e1, b_e1, w_e2, b_e2, norm_f):
    params = [dict(norm1=norm1[l], w_in=w_in[l], s5_lam_re=s5_lam_re[l], s5_lam_im=s5_lam_im[l],
                   s5_log_dt=s5_log_dt[l], s5_b_re=s5_b_re[l], s5_b_im=s5_b_im[l],
                   s5_c_re=s5_c_re[l], s5_c_im=s5_c_im[l], s5_d=s5_d[l], s5_w_glu=s5_w_glu[l],
                   s5_b_glu=s5_b_glu[l], w_s5_out=w_s5_out[l], gla_w_g2=gla_w_g2[l],
                   gla_b_g=gla_b_g[l], gla_norm=gla_norm[l], w_gla_out=w_gla_out[l],
                   w_out=w_out[l], norm2=norm2[l], w_router=w_router[l], b_router=b_router[l],
                   w_e1=w_e1[l], b_e1=b_e1[l], w_e2=w_e2[l], b_e2=b_e2[l])
              for l in range(DEPTH)]

    bsz = x_prompt.shape[0]
    zero_s5 = jnp.zeros((bsz, N_DIRS, S5_GROUPS, S5_STATE), jnp.float32)
    zero_gla = jnp.zeros((bsz, N_DIRS, GLA_HEADS, GLA_DK, GLA_DV), jnp.float32)
    xc = x_prompt
    s5_re_list, s5_im_list, gla_list = [], [], []
    for l in range(DEPTH):
        mod_ctx = (jax.nn.silu(c_ctx) @ w_mod[l] + b_mod[l])[None, None]
        xc, s_re, s_im, s_gla = layer_forward(xc, mod_ctx, zero_s5, zero_s5, zero_gla, params[l])
        s5_re_list.append(s_re)
        s5_im_list.append(s_im)
        gla_list.append(s_gla)
    y_prompt = rms_norm(xc, norm_f)
    new_state_s5_re = jnp.stack(s5_re_list, axis=1)
    new_state_s5_im = jnp.stack(s5_im_list, axis=1)
    new_state_gla = jnp.stack(gla_list, axis=1)

    xs = x_sample + grid_pos_embed(x_sample.shape[1], x_sample.dtype)[None]
    for l in range(DEPTH):
        mod = (jax.nn.silu(c) @ w_mod[l] + b_mod[l])[:, None]
        xs, _, _, _ = layer_forward(xs, mod, state_s5_re[:, l], state_s5_im[:, l],
                                    state_gla[:, l], params[l])
    y_sample = rms_norm(xs, norm_f)
    return (y_prompt, y_sample, new_state_s5_re, new_state_s5_im, new_state_gla)
```

```python
import functools
import math

import jax
import jax.numpy as jnp
from jax import lax
from jax.experimental import pallas as pl
from jax.experimental.pallas import tpu as pltpu

f32 = jnp.float32
bf16 = jnp.bfloat16
i32 = jnp.int32
HIGHEST = lax.Precision.HIGHEST

D_MODEL = 2048
BATCH = 16
SEQ = 256
DEPTH = 2
DEC_BATCH = 2
DEC_SEQ = 2048
GRID_W = 64
N_DIRS = 2
S5_WIDTH = 1024
S5_GROUP = 16
S5_GROUPS = 64
S5_STATE = 64
GLA_HEADS = 4
GLA_DK = 128
GLA_DV = 256
GLA_QK = 512
GLA_V = 1024
GLA_RANK = 16
GLA_GATE_NORM = 16.0
N_EXPERTS = 32
TOP_K = 4
SWIGLU_LIMIT = 7.0
SWIGLU_ALPHA = 1.702
POS_BASE = 10000.0
EPS = 1e-6

N_CTX = BATCH * SEQ
N_LAT = DEC_BATCH * DEC_SEQ
NT = N_CTX + N_LAT
N_SEQS = BATCH + DEC_BATCH
N_MODROWS = 8

LANES = 128
VMEM_LIMIT = 56 * 1024 * 1024

S5_T = 16
S5_ROWS = NT // S5_T
S5_CTX_CHUNKS = SEQ // S5_T
S5_LAT_CHUNKS = DEC_SEQ // S5_T
GLA_C = 128
GLA_SUB = 16
GLA_NSUB = GLA_C // GLA_SUB
GLA_CHUNKS = NT // GLA_C
GLA_CTX_CHUNKS = N_CTX // GLA_C
PRE_TM = 1024
PRE_TN = 1024
PROJ_COLS = 8192
POST_TM = 256
MOE_BM = 128
MOE_ROWS = NT * TOP_K + N_EXPERTS * MOE_BM
MOE_BLOCKS = MOE_ROWS // MOE_BM
MOE_RB = 16
MOE_RMAX = MOE_RB * MOE_BM
MOE_UNITS = N_EXPERTS + MOE_BLOCKS // MOE_RB
MOE_TN = 512
MOE_J1 = D_MODEL // MOE_TN
MOE_J2 = D_MODEL // MOE_TN
GATHER_GB = 256
COMB_TM = 128


def _cparams(sem, **kw):
    return pltpu.CompilerParams(dimension_semantics=sem, vmem_limit_bytes=VMEM_LIMIT, **kw)


def _row_seq_of_tile(i, tile_rows):
    first = i * tile_rows
    return jnp.where(first < N_CTX, 0, 1 + (first - N_CTX) // DEC_SEQ)


def _log_sigmoid(x):
    return -(jnp.maximum(-x, 0.0) + jnp.log1p(jnp.exp(-jnp.abs(x))))


MOD_TN = 1024


def _mod_kernel(c_ref, w_ref, b_ref, o_ref):
    c = c_ref[...]
    s = (c * jax.nn.sigmoid(c)).astype(bf16)
    o_ref[0] = jnp.dot(s, w_ref[0].astype(bf16), preferred_element_type=f32) + b_ref[0]


def _modulation(cond, w_mod, b_mod):
    n_out = 6 * D_MODEL
    return pl.pallas_call(
        _mod_kernel,
        out_shape=jax.ShapeDtypeStruct((DEPTH, N_MODROWS, n_out), f32),
        grid=(DEPTH, n_out // MOD_TN),
        in_specs=[
            pl.BlockSpec((N_MODROWS, D_MODEL), lambda l, j: (0, 0)),
            pl.BlockSpec((1, D_MODEL, MOD_TN), lambda l, j: (l, 0, j)),
            pl.BlockSpec((1, 1, MOD_TN), lambda l, j: (l, 0, j)),
        ],
        out_specs=pl.BlockSpec((1, N_MODROWS, MOD_TN), lambda l, j: (l, 0, j)),
        compiler_params=_cparams(("arbitrary", "arbitrary")),
        name="modulation",
    )(cond, w_mod, b_mod.reshape(DEPTH, 1, n_out))


EMB_TM = 512


def _embed_kernel(xp_ref, xs_ref, pe_ref, o_ref):
    i = pl.program_id(0)

    @pl.when(i < N_CTX // EMB_TM)
    def _():
        o_ref[...] = xp_ref[...]

    @pl.when(i >= N_CTX // EMB_TM)
    def _():
        o_ref[...] = xs_ref[...] + pe_ref[...]


def _embed(x_prompt, x_sample, pe):
    nc = N_CTX // EMB_TM
    per_seq = DEC_SEQ // EMB_TM
    return pl.pallas_call(
        _embed_kernel,
        out_shape=jax.ShapeDtypeStruct((NT, D_MODEL), f32),
        grid=(NT // EMB_TM,),
        in_specs=[
            pl.BlockSpec((EMB_TM, D_MODEL), lambda i: (jnp.minimum(i, nc - 1), 0)),
            pl.BlockSpec((EMB_TM, D_MODEL), lambda i: (jnp.maximum(i - nc, 0), 0)),
            pl.BlockSpec((EMB_TM, D_MODEL), lambda i: (jnp.maximum(i - nc, 0) % per_seq, 0)),
        ],
        out_specs=pl.BlockSpec((EMB_TM, D_MODEL), lambda i: (i, 0)),
        compiler_params=_cparams(("arbitrary",)),
        name="embed",
    )(x_prompt.reshape(N_CTX, D_MODEL), x_sample.reshape(N_LAT, D_MODEL), pe)


def _grid_pos_embed():
    rows = DEC_SEQ // GRID_W
    rr, cc = jnp.meshgrid(jnp.arange(rows, dtype=f32), jnp.arange(GRID_W, dtype=f32), indexing="ij")
    quarter = D_MODEL // 4
    omega = 1.0 / (POS_BASE ** (jnp.arange(quarter, dtype=f32) / quarter))

    def emb(pos):
        ang = pos[:, None] * omega[None, :]
        return jnp.concatenate([jnp.sin(ang), jnp.cos(ang)], axis=-1)

    return jnp.concatenate([emb(rr.reshape(-1)), emb(cc.reshape(-1))], axis=-1)


def _pre_kernel(x_ref, mod_ref, g_ref, w_ref, wglr_ref, wg2_ref, bg_ref, proj_ref, logg_ref, h_ref):
    @pl.when(pl.program_id(1) == 0)
    def _():
        x = x_ref[...]
        y = x * lax.rsqrt(jnp.mean(x * x, axis=-1, keepdims=True) + EPS) * g_ref[...]
        h = y * (1.0 + mod_ref[0, 1:2, :]) + mod_ref[0, 0:1, :]
        hb = h.astype(bf16)
        h_ref[...] = hb
        glr = jnp.dot(hb, wglr_ref[...], preferred_element_type=f32)
        gate = jnp.dot(glr, wg2_ref[...], precision=HIGHEST, preferred_element_type=f32) + bg_ref[...]
        logg_ref[...] = _log_sigmoid(gate) * (1.0 / GLA_GATE_NORM)

    proj_ref[...] = jnp.dot(h_ref[...], w_ref[...], preferred_element_type=f32).astype(bf16)


def _pre(x, mod_l, norm1, w_main, w_glr, w_g2bd, b_g):
    return pl.pallas_call(
        _pre_kernel,
        out_shape=(jax.ShapeDtypeStruct((NT, PROJ_COLS), bf16),
                   jax.ShapeDtypeStruct((NT, N_DIRS * GLA_QK), f32)),
        grid=(NT // PRE_TM, PROJ_COLS // PRE_TN),
        in_specs=[
            pl.BlockSpec((PRE_TM, D_MODEL), lambda i, j: (i, 0)),
            pl.BlockSpec((1, 6, D_MODEL), lambda i, j: (_row_seq_of_tile(i, PRE_TM), 0, 0)),
            pl.BlockSpec((1, D_MODEL), lambda i, j: (0, 0)),
            pl.BlockSpec((D_MODEL, PRE_TN), lambda i, j: (0, j)),
            pl.BlockSpec((D_MODEL, N_DIRS * GLA_RANK), lambda i, j: (0, 0)),
            pl.BlockSpec((N_DIRS * GLA_RANK, N_DIRS * GLA_QK), lambda i, j: (0, 0)),
            pl.BlockSpec((1, N_DIRS * GLA_QK), lambda i, j: (0, 0)),
        ],
        out_specs=(pl.BlockSpec((PRE_TM, PRE_TN), lambda i, j: (i, j)),
                   pl.BlockSpec((PRE_TM, N_DIRS * GLA_QK), lambda i, j: (i, 0))),
        scratch_shapes=[pltpu.VMEM((PRE_TM, D_MODEL), bf16)],
        compiler_params=_cparams(("arbitrary", "arbitrary")),
        name="pre",
    )(x, mod_l, norm1, w_main, w_glr, w_g2bd, b_g)


def _s5_weights(lam_re, lam_im, log_dt, b_re, b_im, c_re, c_im, d):
    T, G, P, H = S5_T, S5_GROUPS, S5_STATE, S5_GROUP
    dt = jnp.exp(log_dt)[..., None]
    z_re, z_im = lam_re * dt, lam_im * dt
    mag = jnp.exp(z_re)
    a_re, a_im = mag * jnp.cos(z_im), mag * jnp.sin(z_im)
    den = lam_re * lam_re + lam_im * lam_im
    n_re, n_im = a_re - 1.0, a_im
    k_re = (n_re * lam_re + n_im * lam_im) / den
    k_im = (n_im * lam_re - n_re * lam_im) / den
    bb_re = k_re[..., None] * b_re - k_im[..., None] * b_im
    bb_im = k_re[..., None] * b_im + k_im[..., None] * b_re
    n = jnp.arange(T + 1, dtype=f32)[:, None, None, None]
    pmag = jnp.exp(n * z_re)
    pw_re, pw_im = pmag * jnp.cos(n * z_im), pmag * jnp.sin(n * z_im)
    abb_re = pw_re[..., None] * bb_re - pw_im[..., None] * bb_im
    abb_im = pw_re[..., None] * bb_im + pw_im[..., None] * bb_re
    kern = (jnp.einsum("dgop,ndgpi->ndgoi", c_re, abb_re[:T], precision=HIGHEST)
            - jnp.einsum("dgop,ndgpi->ndgoi", c_im, abb_im[:T], precision=HIGHEST))

    s_idx = jnp.arange(T)[:, None]
    t_idx = jnp.arange(T)[None, :]
    lag_f = t_idx - s_idx
    kf = jnp.where((lag_f >= 0)[:, :, None, None, None], kern[jnp.clip(lag_f, 0, T - 1), 0], 0.0)
    kb = jnp.where((lag_f <= 0)[:, :, None, None, None], kern[jnp.clip(-lag_f, 0, T - 1), 1], 0.0)
    eye_t = jnp.eye(T, dtype=f32)[:, :, None, None, None]
    eye_h = jnp.eye(H, dtype=f32)[None, None, None, :, :]
    dterm = eye_t * eye_h * d[None, None, :, :, None]
    toeplitz = jnp.transpose(kf + kb + dterm, (2, 0, 4, 1, 3)).reshape(G, T * H, T * H)

    def state_cols(powers, dirn):
        re = jnp.transpose(abb_re[powers, dirn], (1, 0, 3, 2)).reshape(G, T * H, P)
        im = jnp.transpose(abb_im[powers, dirn], (1, 0, 3, 2)).reshape(G, T * H, P)
        return [re, im, im, re]

    w_state = jnp.concatenate(state_cols(T - 1 - jnp.arange(T), 0) + state_cols(jnp.arange(T), 1), axis=-1)

    def carry_rows(powers, dirn):
        pr = pw_re[powers, dirn]
        pi = pw_im[powers, dirn]
        cr, ci = c_re[dirn], c_im[dirn]
        on_re = cr[None] * pr[:, :, None, :] - ci[None] * pi[:, :, None, :]
        on_im = -cr[None] * pi[:, :, None, :] - ci[None] * pr[:, :, None, :]
        to_cols = lambda m: jnp.transpose(m, (1, 3, 0, 2)).reshape(G, P, T * H)
        return [to_cols(on_re), to_cols(on_im)]

    w_carry = jnp.concatenate(carry_rows(1 + jnp.arange(T), 0) + carry_rows(T - jnp.arange(T), 1), axis=1)

    def coef_rows(dirn):
        ar, ai = pw_re[T, dirn], pw_im[T, dirn]
        return [jnp.concatenate([ar, ar], -1), jnp.concatenate([-ai, ai], -1), jnp.concatenate([ai, -ai], -1)]

    zero = jnp.zeros((G, 2 * P), f32)
    coef = jnp.stack(coef_rows(0) + coef_rows(1) + [zero, zero], axis=1)
    return toeplitz.astype(bf16), w_state.astype(bf16), w_carry.astype(bf16), coef


def _s5_kernel(u_ref, tz_ref, ws_ref, wc_ref, coef_ref, h0_ref, y_ref, fin_ref, s_ref, hin_ref):
    u = u_ref[0]
    s_ref[...] = jnp.dot(u, ws_ref[0], preferred_element_type=f32)
    co = coef_ref[0]
    a1f, a2f, a2sf = co[0:1], co[1:2], co[2:3]
    a1b, a2b, a2sb = co[3:4], co[4:5], co[5:6]
    P2 = 2 * S5_STATE

    def step(rf, rb, n_rows, carry):
        hf, hfs, hb, hbs = carry
        hin_ref[pl.ds(rf, n_rows), 0:P2] = hf
        hin_ref[pl.ds(rb, n_rows), P2:2 * P2] = hb
        sf = s_ref[pl.ds(rf, n_rows), 0:P2]
        sfs = s_ref[pl.ds(rf, n_rows), P2:2 * P2]
        sb = s_ref[pl.ds(rb, n_rows), 2 * P2:3 * P2]
        sbs = s_ref[pl.ds(rb, n_rows), 3 * P2:4 * P2]
        return (a1f * hf + a2f * hfs + sf, a1f * hfs + a2sf * hf + sfs,
                a1b * hb + a2b * hbs + sb, a1b * hbs + a2sb * hb + sbs)

    zero = jnp.zeros((BATCH, P2), f32)
    carry = (zero, zero, zero, zero)
    for c in range(S5_CTX_CHUNKS):
        carry = step(c * BATCH, (S5_CTX_CHUNKS - 1 - c) * BATCH, BATCH, carry)
    fin_ref[0] = jnp.concatenate([carry[0], carry[2]], axis=1)

    h0 = h0_ref[0]
    lat0 = S5_CTX_CHUNKS * BATCH

    carry = (h0[:, 0:P2], h0[:, P2:2 * P2], h0[:, 2 * P2:3 * P2], h0[:, 3 * P2:4 * P2])
    for c in range(S5_LAT_CHUNKS):
        carry = step(lat0 + c * DEC_BATCH, lat0 + (S5_LAT_CHUNKS - 1 - c) * DEC_BATCH, DEC_BATCH, carry)

    y = jnp.dot(u, tz_ref[0], preferred_element_type=f32)
    y = y + jnp.dot(hin_ref[...].astype(bf16), wc_ref[0], preferred_element_type=f32)
    y_ref[0] = y.astype(bf16)


def _s5_scan(u_g, toeplitz, w_state, w_carry, coef, h0):
    G, TH, P2 = S5_GROUPS, S5_T * S5_GROUP, 2 * S5_STATE
    return pl.pallas_call(
        _s5_kernel,
        out_shape=(jax.ShapeDtypeStruct((G, S5_ROWS, TH), bf16),
                   jax.ShapeDtypeStruct((G, BATCH, 2 * P2), f32)),
        grid=(G,),
        in_specs=[
            pl.BlockSpec((1, S5_ROWS, TH), lambda g: (g, 0, 0)),
            pl.BlockSpec((1, TH, TH), lambda g: (g, 0, 0)),
            pl.BlockSpec((1, TH, 4 * P2), lambda g: (g, 0, 0)),
            pl.BlockSpec((1, 2 * P2, TH), lambda g: (g, 0, 0)),
            pl.BlockSpec((1, 8, P2), lambda g: (g, 0, 0)),
            pl.BlockSpec((1, DEC_BATCH, 4 * P2), lambda g: (g, 0, 0)),
        ],
        out_specs=(pl.BlockSpec((1, S5_ROWS, TH), lambda g: (g, 0, 0)),
                   pl.BlockSpec((1, BATCH, 2 * P2), lambda g: (g, 0, 0))),
        scratch_shapes=[pltpu.VMEM((S5_ROWS, 4 * P2), f32), pltpu.VMEM((S5_ROWS, 2 * P2), f32)],
        compiler_params=_cparams(("arbitrary",)),
        name="s5_scan",
    )(u_g, toeplitz, w_state, w_carry, coef, h0)


def _s5_to_groups(u):
    G, H, T = S5_GROUPS, S5_GROUP, S5_T
    ctx = u[:N_CTX].reshape(BATCH, S5_CTX_CHUNKS, T, G, H)
    lat = u[N_CTX:].reshape(DEC_BATCH, S5_LAT_CHUNKS, T, G, H)
    ctx = jnp.transpose(ctx, (3, 1, 0, 2, 4)).reshape(G, S5_CTX_CHUNKS * BATCH, T * H)
    lat = jnp.transpose(lat, (3, 1, 0, 2, 4)).reshape(G, S5_LAT_CHUNKS * DEC_BATCH, T * H)
    return jnp.concatenate([ctx, lat], axis=1)


def _s5_from_groups(y):
    G, H, T = S5_GROUPS, S5_GROUP, S5_T
    nc = S5_CTX_CHUNKS * BATCH
    ctx = y[:, :nc].reshape(G, S5_CTX_CHUNKS, BATCH, T, H)
    lat = y[:, nc:].reshape(G, S5_LAT_CHUNKS, DEC_BATCH, T, H)
    ctx = jnp.transpose(ctx, (2, 1, 3, 0, 4)).reshape(N_CTX, G * H)
    lat = jnp.transpose(lat, (2, 1, 3, 0, 4)).reshape(N_LAT, G * H)
    return jnp.concatenate([ctx, lat], axis=0)


def _s5_init_rows(s_re, s_im):
    parts = [s_re[:, 0], s_im[:, 0], s_im[:, 0], s_re[:, 0], s_re[:, 1], s_im[:, 1], s_im[:, 1], s_re[:, 1]]
    return jnp.transpose(jnp.concatenate(parts, axis=-1), (1, 0, 2))


def _gla_seq_of_chunk(cj):
    per_ctx = SEQ // GLA_C
    per_lat = DEC_SEQ // GLA_C
    return jnp.where(cj < GLA_CTX_CHUNKS, cj // per_ctx, BATCH + (cj - GLA_CTX_CHUNKS) // per_lat)


def _gla_kernel(q_ref, k_ref, v_ref, lg_ref, s0_ref, o_ref, st_ref, a_ref, *, reverse):
    j = pl.program_id(0)
    cj = (GLA_CHUNKS - 1 - j) if reverse else j
    per_ctx = SEQ // GLA_C
    per_lat = DEC_SEQ // GLA_C
    pos = jnp.where(cj < GLA_CTX_CHUNKS, cj % per_ctx, (cj - GLA_CTX_CHUNKS) % per_lat)
    n_in_seq = jnp.where(cj < GLA_CTX_CHUNKS, per_ctx, per_lat)
    is_first = (pos == n_in_seq - 1) if reverse else (pos == 0)

    @pl.when(is_first)
    def _():
        st_ref[...] = s0_ref[...]

    C, SUB = GLA_C, GLA_SUB
    row = lax.broadcasted_iota(i32, (C, C), 0)
    col = lax.broadcasted_iota(i32, (C, C), 1)
    if reverse:
        m_cum = (col >= row)
        m_ref = (col >= (row // SUB + 1) * SUB)
    else:
        m_cum = (col <= row)
        m_ref = (col < (row // SUB) * SUB)
    m_both = jnp.concatenate([m_cum.astype(f32), m_ref.astype(f32)], axis=0)
    lg_all = lg_ref[...]
    sums = jnp.dot(m_both, lg_all, precision=HIGHEST, preferred_element_type=f32)
    scale = GLA_DK ** -0.5
    a_ref[...] = jnp.zeros((C, C), f32)

    for h in range(GLA_HEADS):
        ks = slice(h * GLA_DK, (h + 1) * GLA_DK)
        vs = slice(h * GLA_DV, (h + 1) * GLA_DV)
        lg = lg_all[:, ks]
        cum = sums[0:C, ks]
        ref = sums[C:2 * C, ks]
        tot = jnp.sum(lg, axis=0, keepdims=True)
        q = q_ref[:, ks].astype(f32) * scale
        k = k_ref[:, ks].astype(f32)
        v = v_ref[:, vs]
        state = st_ref[0, h]

        q_in = (q * jnp.exp(cum)).astype(bf16)
        o = jnp.dot(q_in, state.astype(bf16), preferred_element_type=f32)
        k_st = k * jnp.exp(tot - cum)
        tot_col = jnp.sum(lg.T, axis=1, keepdims=True)
        st_ref[0, h] = jnp.exp(tot_col) * state + jnp.dot(k_st.T.astype(bf16), v, preferred_element_type=f32)

        qt = (q * jnp.exp(cum - ref)).astype(bf16)
        for blk in range(GLA_NSUB):
            rows = slice(blk * SUB, (blk + 1) * SUB)
            cols = slice((blk + 1) * SUB, C) if reverse else slice(0, blk * SUB)
            if cols.start == cols.stop:
                continue
            kt = (k[cols] * jnp.exp(ref[blk * SUB:blk * SUB + 1] - cum[cols])).astype(bf16)
            a_ref[rows, cols] = lax.dot_general(qt[rows], kt, (((1,), (1,)), ((), ())), preferred_element_type=f32)

        diag = jnp.zeros((C, C), f32)
        for lag in range(SUB):
            shift = (C - lag) % C if reverse else lag
            k_sh = k if lag == 0 else pltpu.roll(k, shift, axis=0)
            c_sh = cum if lag == 0 else pltpu.roll(cum, shift, axis=0)
            z = q * k_sh * jnp.exp(jnp.minimum(cum - c_sh, 0.0))
            r = jnp.sum(z, axis=1, keepdims=True)
            if reverse:
                hit = (col == row + lag) & (row % SUB + lag < SUB)
            else:
                hit = (col == row - lag) & (row % SUB >= lag)
            diag = diag + jnp.where(hit, r, 0.0)
        scores = (a_ref[...] + diag).astype(bf16)
        o_ref[:, vs] = o + jnp.dot(scores, v, preferred_element_type=f32)


def _gla_direction(proj, logg, s0, reverse):
    dirn = 1 if reverse else 0
    cidx = (lambda j: GLA_CHUNKS - 1 - j) if reverse else (lambda j: j)
    q_blk = (S5_WIDTH) // GLA_QK
    v_blk = (S5_WIDTH + 2 * GLA_QK) // GLA_V
    return pl.pallas_call(
        functools.partial(_gla_kernel, reverse=reverse),
        out_shape=(jax.ShapeDtypeStruct((NT, GLA_V), f32),
                   jax.ShapeDtypeStruct((N_SEQS, GLA_HEADS, GLA_DK, GLA_DV), f32)),
        grid=(GLA_CHUNKS,),
        in_specs=[
            pl.BlockSpec((GLA_C, GLA_QK), lambda j: (cidx(j), q_blk)),
            pl.BlockSpec((GLA_C, GLA_QK), lambda j: (cidx(j), q_blk + 1)),
            pl.BlockSpec((GLA_C, GLA_V), lambda j: (cidx(j), v_blk)),
            pl.BlockSpec((GLA_C, GLA_QK), lambda j: (cidx(j), dirn)),
            pl.BlockSpec((1, GLA_HEADS, GLA_DK, GLA_DV), lambda j: (_gla_seq_of_chunk(cidx(j)), 0, 0, 0)),
        ],
        out_specs=(pl.BlockSpec((GLA_C, GLA_V), lambda j: (cidx(j), 0)),
                   pl.BlockSpec((1, GLA_HEADS, GLA_DK, GLA_DV), lambda j: (_gla_seq_of_chunk(cidx(j)), 0, 0, 0))),
        scratch_shapes=[pltpu.VMEM((GLA_C, GLA_C), f32)],
        compiler_params=_cparams(("arbitrary",)),
        name="gla_bwd" if reverse else "gla_fwd",
    )(proj, proj, proj, logg, s0)


def _post_kernel(y_ref, of_ref, ob_ref, r_ref, ga_ref, gb_ref, x_ref, mod_ref,
                 wglu_ref, bglu_ref, ws5_ref, gn_ref, wgla_ref, wout_ref, n2_ref, wr_ref, br_ref,
                 x1_ref, h2_ref, gate_ref, idx_ref):
    g = jax.nn.gelu(y_ref[...].astype(f32))
    glu = jnp.dot(g.astype(bf16), wglu_ref[...], preferred_element_type=f32) + bglu_ref[...]
    ya = g * jax.nn.sigmoid(glu)

    o = of_ref[...] + ob_ref[...]
    heads = []
    for h in range(GLA_HEADS):
        oh = o[:, h * GLA_DV:(h + 1) * GLA_DV]
        heads.append(oh * lax.rsqrt(jnp.mean(oh * oh, axis=-1, keepdims=True) + EPS))
    r = r_ref[...].astype(f32)
    yb = jnp.concatenate(heads, axis=1) * gn_ref[...] * (r * jax.nn.sigmoid(r))

    merged = (jax.nn.sigmoid(ga_ref[...].astype(f32)) * jnp.dot(ya.astype(bf16), ws5_ref[...], preferred_element_type=f32)
              + jax.nn.sigmoid(gb_ref[...].astype(f32)) * jnp.dot(yb.astype(bf16), wgla_ref[...], preferred_element_type=f32))
    x1 = x_ref[...] + mod_ref[0, 2:3, :] * jnp.dot(merged.astype(bf16), wout_ref[...], preferred_element_type=f32)
    x1_ref[...] = x1

    y2 = x1 * lax.rsqrt(jnp.mean(x1 * x1, axis=-1, keepdims=True) + EPS) * n2_ref[...]
    h2 = y2 * (1.0 + mod_ref[0, 4:5, :]) + mod_ref[0, 3:4, :]
    h2_ref[...] = h2

    logits = jnp.dot(h2, wr_ref[...], precision=HIGHEST, preferred_element_type=f32) + br_ref[...]
    lane = lax.broadcasted_iota(i32, logits.shape, 1)
    lane_f = lane.astype(f32)
    vals = logits
    top_v, top_i = [], []
    for _ in range(TOP_K):
        m = jnp.max(vals, axis=-1, keepdims=True)
        idx = jnp.min(jnp.where(vals == m, lane_f, float(LANES)), axis=-1, keepdims=True).astype(i32)
        top_v.append(m)
        top_i.append(idx)
        vals = jnp.where(lane == idx, -jnp.inf, vals)
    ex = [jnp.exp(v - top_v[0]) for v in top_v]
    inv = 1.0 / (ex[0] + ex[1] + ex[2] + ex[3])
    gates = jnp.zeros(logits.shape, f32)
    idxs = jnp.zeros(logits.shape, i32)
    for kk in range(TOP_K):
        gates = jnp.where(lane == kk, ex[kk] * inv, gates)
        idxs = jnp.where(lane == kk, top_i[kk], idxs)
    gate_ref[...] = gates
    idx_ref[...] = idxs


def _post(y_s5, o_f, o_b, proj, x, mod_l, wglu, bglu, ws5, gnorm, wgla, wout, norm2, wr, br):
    const = lambda shape: pl.BlockSpec(shape, lambda i: (0,) * len(shape), pipeline_mode=pl.Buffered(1))
    r_blk = (S5_WIDTH + 2 * GLA_QK + GLA_V) // GLA_V
    ga_blk = 4096 // D_MODEL
    return pl.pallas_call(
        _post_kernel,
        out_shape=(jax.ShapeDtypeStruct((NT, D_MODEL), f32),
                   jax.ShapeDtypeStruct((NT, D_MODEL), f32),
                   jax.ShapeDtypeStruct((NT, LANES), f32),
                   jax.ShapeDtypeStruct((NT, LANES), i32)),
        grid=(NT // POST_TM,),
        in_specs=[
            pl.BlockSpec((POST_TM, S5_WIDTH), lambda i: (i, 0)),
            pl.BlockSpec((POST_TM, GLA_V), lambda i: (i, 0)),
            pl.BlockSpec((POST_TM, GLA_V), lambda i: (i, 0)),
            pl.BlockSpec((POST_TM, GLA_V), lambda i: (i, r_blk)),
            pl.BlockSpec((POST_TM, D_MODEL), lambda i: (i, ga_blk)),
            pl.BlockSpec((POST_TM, D_MODEL), lambda i: (i, ga_blk + 1)),
            pl.BlockSpec((POST_TM, D_MODEL), lambda i: (i, 0)),
            pl.BlockSpec((1, 6, D_MODEL), lambda i: (_row_seq_of_tile(i, POST_TM), 0, 0)),
            const((S5_WIDTH, S5_WIDTH)), const((1, S5_WIDTH)), const((S5_WIDTH, D_MODEL)),
            const((1, GLA_V)), const((GLA_V, D_MODEL)), const((D_MODEL, D_MODEL)), const((1, D_MODEL)),
            const((D_MODEL, LANES)), const((1, LANES)),
        ],
        out_specs=(pl.BlockSpec((POST_TM, D_MODEL), lambda i: (i, 0)),
                   pl.BlockSpec((POST_TM, D_MODEL), lambda i: (i, 0)),
                   pl.BlockSpec((POST_TM, LANES), lambda i: (i, 0)),
                   pl.BlockSpec((POST_TM, LANES), lambda i: (i, 0))),
        compiler_params=_cparams(("arbitrary",)),
        name="post",
    )(y_s5, o_f, o_b, proj, proj, proj, x, mod_l, wglu, bglu, ws5, gnorm, wgla, wout, norm2, wr, br)


def _routing(top_idx):
    m = NT * TOP_K
    e_flat = top_idx.reshape(-1)
    order = jnp.argsort(e_flat, stable=True)
    e_sorted = e_flat[order]
    counts = jnp.bincount(e_flat, length=N_EXPERTS).astype(i32)
    starts = jnp.cumsum(counts) - counts
    nblk = (counts + MOE_BM - 1) // MOE_BM
    blk_end = jnp.cumsum(nblk)
    blk_start = blk_end - nblk
    dest_sorted = blk_start[e_sorted] * MOE_BM + (jnp.arange(m, dtype=i32) - starts[e_sorted])
    row_tok = jnp.zeros((MOE_ROWS,), i32).at[dest_sorted].set((order // TOP_K).astype(i32))
    dest = jnp.zeros((m,), i32).at[order].set(dest_sorted)

    n_units = (nblk + MOE_RB - 1) // MOE_RB
    unit_end = jnp.cumsum(n_units)
    unit_start = unit_end - n_units
    total_units = unit_end[-1]
    u = jnp.arange(MOE_UNITS, dtype=i32)
    used = u < total_units
    e_of_u = jnp.minimum(jnp.searchsorted(unit_end, u, side="right"), N_EXPERTS - 1).astype(i32)
    local = u - unit_start[e_of_u]
    u_blk0 = blk_start[e_of_u] + local * MOE_RB
    u_nblk = jnp.minimum(MOE_RB, nblk[e_of_u] - local * MOE_RB)
    last_e = e_of_u[jnp.maximum(total_units - 1, 0)]
    u_exp = jnp.where(used, e_of_u, last_e).astype(i32)
    u_blk0 = jnp.where(used, u_blk0, 0).astype(i32)
    u_nblk = jnp.where(used, u_nblk, 0).astype(i32)
    return row_tok, dest, u_exp, u_blk0, u_nblk, blk_end[-1:].astype(i32)


def _gather_kernel(tok_ref, src_ref, o_ref, stage_ref, sem):
    base = pl.program_id(0) * GATHER_GB

    def row_copy(t, r):
        return pltpu.make_async_copy(src_ref.at[pl.ds(t, 1), :], stage_ref.at[pl.ds(r, 1), :], sem)

    def issue(r, c):
        row_copy(tok_ref[base + r], r).start()
        return c

    lax.fori_loop(0, GATHER_GB, issue, 0)

    def wait(r, c):
        row_copy(0, r).wait()
        return c

    lax.fori_loop(0, GATHER_GB, wait, 0)
    o_ref[...] = stage_ref[...].astype(bf16)


def _gather_rows(h2, row_tok):
    return pl.pallas_call(
        _gather_kernel,
        out_shape=jax.ShapeDtypeStruct((MOE_ROWS, D_MODEL), bf16),
        grid_spec=pltpu.PrefetchScalarGridSpec(
            num_scalar_prefetch=1,
            grid=(MOE_ROWS // GATHER_GB,),
            in_specs=[pl.BlockSpec(memory_space=pl.ANY)],
            out_specs=pl.BlockSpec((GATHER_GB, D_MODEL), lambda i, tok: (i, 0)),
            scratch_shapes=[pltpu.VMEM((GATHER_GB, D_MODEL), f32), pltpu.SemaphoreType.DMA(())],
        ),
        compiler_params=_cparams(("arbitrary",)),
        name="moe_gather",
    )(row_tok, h2)


def _moe_kernel(uexp_ref, ublk_ref, unb_ref, nused_ref, x_hbm, w1g_ref, w1l_ref, b1g_ref, b1l_ref, w2_ref, b2_ref,
                y_hbm, x_buf, act_buf, w2_buf, y_buf, zero_buf, sem_in, sem_out):
    u = pl.program_id(0)
    j = pl.program_id(1)
    nb = unb_ref[u]
    row0 = ublk_ref[u] * MOE_BM

    def in_copy(r):
        rows = pl.ds(pl.multiple_of(r * MOE_BM, MOE_BM), MOE_BM)
        src = x_hbm.at[pl.ds(pl.multiple_of(row0 + r * MOE_BM, MOE_BM), MOE_BM), :]
        return pltpu.make_async_copy(src, x_buf.at[rows, :], sem_in)

    @pl.when((j == 0) & (nb > 0))
    def _():
        lax.fori_loop(0, nb, lambda r, c: (in_copy(r).start(), c)[1], 0)
        lax.fori_loop(0, nb, lambda r, c: (in_copy(r).wait(), c)[1], 0)

    @pl.when((j < MOE_J1) & (nb > 0))
    def _():
        def body(r, c):
            rows = pl.ds(pl.multiple_of(r * MOE_BM, MOE_BM), MOE_BM)
            x = x_buf[rows, :]
            hg = jnp.dot(x, w1g_ref[0], preferred_element_type=f32) + b1g_ref[0]
            hl = jnp.dot(x, w1l_ref[0], preferred_element_type=f32) + b1l_ref[0]
            hg = jnp.minimum(hg, SWIGLU_LIMIT)
            hl = jnp.clip(hl, -SWIGLU_LIMIT, SWIGLU_LIMIT)
            act_buf[j, rows, :] = (hg * jax.nn.sigmoid(SWIGLU_ALPHA * hg) * (hl + 1.0)).astype(bf16)
            return c

        lax.fori_loop(0, nb, body, 0)

    @pl.when((j >= MOE_J1) & (nb > 0))
    def _():
        w2_buf[...] = w2_ref[0].astype(bf16)
        col0 = pl.multiple_of((j - MOE_J1) * MOE_TN, MOE_TN)

        def out_copy(r):
            rows = pl.ds(pl.multiple_of(r * MOE_BM, MOE_BM), MOE_BM)
            dst = y_hbm.at[pl.ds(pl.multiple_of(row0 + r * MOE_BM, MOE_BM), MOE_BM), pl.ds(col0, MOE_TN)]
            return pltpu.make_async_copy(y_buf.at[rows, :], dst, sem_out)

        def body(r, c):
            rows = pl.ds(pl.multiple_of(r * MOE_BM, MOE_BM), MOE_BM)
            acc = b2_ref[0] + jnp.zeros((MOE_BM, MOE_TN), f32)
            for jj in range(MOE_J1):
                acc = acc + jnp.dot(act_buf[jj, rows, :], w2_buf[jj * MOE_TN:(jj + 1) * MOE_TN, :],
                                    preferred_element_type=f32)
            y_buf[rows, :] = acc
            out_copy(r).start()
            return c

        lax.fori_loop(0, nb, body, 0)
        lax.fori_loop(0, nb, lambda r, c: (out_copy(r).wait(), c)[1], 0)

    @pl.when((u == MOE_UNITS - 1) & (j == MOE_J1 + MOE_J2 - 1))
    def _():
        zero_buf[...] = jnp.zeros(zero_buf.shape, f32)

        def fill_copy(r):
            dst = y_hbm.at[pl.ds(pl.multiple_of(r * MOE_BM, MOE_BM), MOE_BM), :]
            return pltpu.make_async_copy(zero_buf, dst, sem_out)

        lax.fori_loop(nused_ref[0], MOE_BLOCKS, lambda r, c: (fill_copy(r).start(), c)[1], 0)
        lax.fori_loop(nused_ref[0], MOE_BLOCKS, lambda r, c: (fill_copy(r).wait(), c)[1], 0)


def _moe_experts(x_sorted, u_exp, u_blk0, u_nblk, n_used, w1g, w1l, b1g, b1l, w2, b2):
    def w1_idx(u, j, uexp, ublk, unb, nused):
        return (uexp[u], 0, jnp.where(unb[u] > 0, jnp.minimum(j, MOE_J1 - 1), MOE_J1 - 1))

    def w2_idx(u, j, uexp, ublk, unb, nused):
        return (uexp[u], 0, jnp.where(unb[u] > 0, jnp.maximum(j - MOE_J1, 0), MOE_J2 - 1))

    return pl.pallas_call(
        _moe_kernel,
        out_shape=jax.ShapeDtypeStruct((MOE_ROWS, D_MODEL), f32),
        grid_spec=pltpu.PrefetchScalarGridSpec(
            num_scalar_prefetch=4,
            grid=(MOE_UNITS, MOE_J1 + MOE_J2),
            in_specs=[
                pl.BlockSpec(memory_space=pl.ANY),
                pl.BlockSpec((1, D_MODEL, MOE_TN), w1_idx),
                pl.BlockSpec((1, D_MODEL, MOE_TN), w1_idx),
                pl.BlockSpec((1, 1, MOE_TN), w1_idx),
                pl.BlockSpec((1, 1, MOE_TN), w1_idx),
                pl.BlockSpec((1, D_MODEL, MOE_TN), w2_idx),
                pl.BlockSpec((1, 1, MOE_TN), w2_idx),
            ],
            out_specs=pl.BlockSpec(memory_space=pl.ANY),
            scratch_shapes=[
                pltpu.VMEM((MOE_RMAX, D_MODEL), bf16),
                pltpu.VMEM((MOE_J1, MOE_RMAX, MOE_TN), bf16),
                pltpu.VMEM((D_MODEL, MOE_TN), bf16),
                pltpu.VMEM((MOE_RMAX, MOE_TN), f32),
                pltpu.VMEM((MOE_BM, D_MODEL), f32),
                pltpu.SemaphoreType.DMA(()),
                pltpu.SemaphoreType.DMA(()),
            ],
        ),
        compiler_params=_cparams(("arbitrary", "arbitrary")),
        name="moe_experts",
    )(u_exp, u_blk0, u_nblk, n_used, x_sorted, w1g, w1l, b1g, b1l, w2, b2)


def _combine_kernel(dest_ref, y_hbm, gate_ref, x1_ref, mod_ref, nf_ref, o_ref, y_buf, sem, *, final_norm):
    base = pl.program_id(0) * COMB_TM * TOP_K

    def row_copy(p, kk, r):
        return pltpu.make_async_copy(y_hbm.at[pl.ds(p, 1), :], y_buf.at[kk, pl.ds(r, 1), :], sem)

    def issue(r, c):
        for kk in range(TOP_K):
            row_copy(dest_ref[base + r * TOP_K + kk], kk, r).start()
        return c

    lax.fori_loop(0, COMB_TM, issue, 0)

    def wait(r, c):
        for kk in range(TOP_K):
            row_copy(0, kk, r).wait()
        return c

    lax.fori_loop(0, COMB_TM, wait, 0)

    gates = gate_ref[...]
    acc = jnp.zeros((COMB_TM, D_MODEL), f32)
    for kk in range(TOP_K):
        acc = acc + gates[:, kk:kk + 1] * y_buf[kk]
    x2 = x1_ref[...] + mod_ref[0, 5:6, :] * acc
    if final_norm:
        x2 = x2 * lax.rsqrt(jnp.mean(x2 * x2, axis=-1, keepdims=True) + EPS) * nf_ref[...]
    o_ref[...] = x2


def _combine(y_sorted, dest, gates, x1, mod_l, norm_f, final_norm):
    return pl.pallas_call(
        functools.partial(_combine_kernel, final_norm=final_norm),
        out_shape=jax.ShapeDtypeStruct((NT, D_MODEL), f32),
        grid_spec=pltpu.PrefetchScalarGridSpec(
            num_scalar_prefetch=1,
            grid=(NT // COMB_TM,),
            in_specs=[
                pl.BlockSpec(memory_space=pl.ANY),
                pl.BlockSpec((COMB_TM, LANES), lambda i, d: (i, 0)),
                pl.BlockSpec((COMB_TM, D_MODEL), lambda i, d: (i, 0)),
                pl.BlockSpec((1, 6, D_MODEL), lambda i, d: (_row_seq_of_tile(i, COMB_TM), 0, 0)),
                pl.BlockSpec((1, D_MODEL), lambda i, d: (0, 0)),
            ],
            out_specs=pl.BlockSpec((COMB_TM, D_MODEL), lambda i, d: (i, 0)),
            scratch_shapes=[pltpu.VMEM((TOP_K, COMB_TM, D_MODEL), f32), pltpu.SemaphoreType.DMA(())],
        ),
        compiler_params=_cparams(("arbitrary",)),
        name="moe_combine_final" if final_norm else "moe_combine",
    )(dest, y_sorted, gates, x1, mod_l, norm_f)


def kernel(x_prompt, x_sample, c, state_s5_re, state_s5_im, state_gla, c_ctx, w_mod, b_mod, norm1, w_in, s5_lam_re, s5_lam_im, s5_log_dt, s5_b_re, s5_b_im, s5_c_re, s5_c_im, s5_d, s5_w_glu, s5_b_glu, w_s5_out, gla_w_g2, gla_b_g, gla_norm, w_gla_out, w_out, norm2, w_router, b_router, w_e1, b_e1, w_e2, b_e2, norm_f):
    cond = jnp.concatenate([c_ctx[None], c, jnp.zeros((N_MODROWS - 1 - DEC_BATCH, D_MODEL), f32)], axis=0)
    mod = _modulation(cond, w_mod, b_mod).reshape(DEPTH, N_MODROWS, 6, D_MODEL)
    x = _embed(x_prompt, x_sample, _grid_pos_embed())

    glr0 = S5_WIDTH + 2 * GLA_QK + 2 * GLA_V
    glr1 = glr0 + N_DIRS * GLA_RANK
    s5_re_out, s5_im_out, gla_out = [], [], []
    for l in range(DEPTH):
        mod_l = mod[l]
        w_main = jnp.concatenate([w_in[l, :, :glr0], w_in[l, :, glr1:]], axis=1).astype(bf16)
        w_glr = w_in[l, :, glr0:glr1].astype(bf16)
        zg = jnp.zeros((GLA_RANK, GLA_QK), f32)
        w_g2bd = jnp.concatenate([jnp.concatenate([gla_w_g2[l, 0], zg], axis=1),
                                  jnp.concatenate([zg, gla_w_g2[l, 1]], axis=1)], axis=0)
        b_g = gla_b_g[l].reshape(1, N_DIRS * GLA_QK)
        toeplitz, w_state, w_carry, coef = _s5_weights(
            s5_lam_re[l], s5_lam_im[l], s5_log_dt[l], s5_b_re[l], s5_b_im[l], s5_c_re[l], s5_c_im[l], s5_d[l])
        h0 = _s5_init_rows(state_s5_re[:, l], state_s5_im[:, l])
        zero_state = jnp.zeros((BATCH, GLA_HEADS, GLA_DK, GLA_DV), f32)
        w_r = jnp.concatenate([w_router[l], jnp.zeros((D_MODEL, LANES - N_EXPERTS), f32)], axis=1)
        b_r = jnp.concatenate([b_router[l], jnp.full((LANES - N_EXPERTS,), -jnp.inf, f32)]).reshape(1, LANES)
        w1g = w_e1[l, :, :, 0::2].astype(bf16)
        w1l = w_e1[l, :, :, 1::2].astype(bf16)
        b1g = b_e1[l, :, 0::2].reshape(N_EXPERTS, 1, D_MODEL)
        b1l = b_e1[l, :, 1::2].reshape(N_EXPERTS, 1, D_MODEL)
        b2 = b_e2[l].reshape(N_EXPERTS, 1, D_MODEL)

        proj, logg = _pre(x, mod_l, norm1[l].reshape(1, D_MODEL), w_main, w_glr, w_g2bd, b_g)
        y_g, s5_fin = _s5_scan(_s5_to_groups(proj[:, :S5_WIDTH]), toeplitz, w_state, w_carry, coef, h0)
        y_s5 = _s5_from_groups(y_g)
        o_f, st_f = _gla_direction(proj, logg, jnp.concatenate([zero_state, state_gla[:, l, 0]], axis=0), False)
        o_b, st_b = _gla_direction(proj, logg, jnp.concatenate([zero_state, state_gla[:, l, 1]], axis=0), True)

        x1, h2, gates, top_idx = _post(
            y_s5, o_f, o_b, proj, x, mod_l,
            s5_w_glu[l].astype(bf16), s5_b_glu[l].reshape(1, S5_WIDTH), w_s5_out[l].astype(bf16),
            jnp.tile(gla_norm[l], GLA_HEADS).reshape(1, GLA_V), w_gla_out[l].astype(bf16), w_out[l].astype(bf16),
            norm2[l].reshape(1, D_MODEL), w_r, b_r)

        row_tok, dest, u_exp, u_blk0, u_nblk, n_used = _routing(top_idx[:, :TOP_K])
        x_sorted = _gather_rows(h2, row_tok)
        y_sorted = _moe_experts(x_sorted, u_exp, u_blk0, u_nblk, n_used, w1g, w1l, b1g, b1l, w_e2[l], b2)
        x = _combine(y_sorted, dest, gates, x1, mod_l, norm_f.reshape(1, D_MODEL), l == DEPTH - 1)

        P = S5_STATE
        fin = jnp.transpose(s5_fin, (1, 0, 2))
        s5_re_out.append(jnp.stack([fin[..., 0:P], fin[..., 2 * P:3 * P]], axis=1))
        s5_im_out.append(jnp.stack([fin[..., P:2 * P], fin[..., 3 * P:4 * P]], axis=1))
        gla_out.append(jnp.stack([st_f[:BATCH], st_b[:BATCH]], axis=1))

    y_prompt = x[:N_CTX].reshape(BATCH, SEQ, D_MODEL)
    y_sample = x[N_CTX:].reshape(DEC_BATCH, DEC_SEQ, D_MODEL)
    return (y_prompt, y_sample, jnp.stack(s5_re_out, axis=1), jnp.stack(s5_im_out, axis=1),
            jnp.stack(gla_out, axis=1))
```

```python
import functools
import math

import jax
import jax.numpy as jnp
from jax import lax
from jax.experimental import pallas as pl
from jax.experimental.pallas import tpu as pltpu

f32 = jnp.float32
bf16 = jnp.bfloat16
i32 = jnp.int32
HIGHEST = lax.Precision.HIGHEST

D_MODEL = 2048
BATCH = 16
SEQ = 256
DEPTH = 2
DEC_BATCH = 2
DEC_SEQ = 2048
GRID_W = 64
N_DIRS = 2
S5_WIDTH = 1024
S5_GROUP = 16
S5_GROUPS = 64
S5_STATE = 64
GLA_HEADS = 4
GLA_DK = 128
GLA_DV = 256
GLA_QK = 512
GLA_V = 1024
GLA_RANK = 16
GLA_GATE_NORM = 16.0
N_EXPERTS = 32
TOP_K = 4
SWIGLU_LIMIT = 7.0
SWIGLU_ALPHA = 1.702
POS_BASE = 10000.0
EPS = 1e-6

N_CTX = BATCH * SEQ
N_LAT = DEC_BATCH * DEC_SEQ
NT = N_CTX + N_LAT
N_SEQS = BATCH + DEC_BATCH
N_MODROWS = 8

LANES = 128
VMEM_LIMIT = 56 * 1024 * 1024

S5_T = 16
S5_ROWS = NT // S5_T
S5_CTX_CHUNKS = SEQ // S5_T
S5_LAT_CHUNKS = DEC_SEQ // S5_T
GLA_C = 128
GLA_SUB = 16
GLA_NSUB = GLA_C // GLA_SUB
GLA_CHUNKS = NT // GLA_C
GLA_CTX_CHUNKS = N_CTX // GLA_C
PRE_TM = 1024
PRE_TN = 1024
PROJ_COLS = 8192
POST_TM = 256
MOE_BM = 128
MOE_ROWS = NT * TOP_K + N_EXPERTS * MOE_BM
MOE_BLOCKS = MOE_ROWS // MOE_BM
MOE_RB = 16
MOE_RMAX = MOE_RB * MOE_BM
MOE_UNITS = N_EXPERTS + MOE_BLOCKS // MOE_RB
MOE_TN = 512
MOE_TA = MOE_TN // 2
MOE_J1 = 2 * D_MODEL // MOE_TN
MOE_J2 = D_MODEL // MOE_TN
GATHER_GB = 256
COMB_TM = 128


def _cparams(sem, **kw):
    return pltpu.CompilerParams(dimension_semantics=sem, vmem_limit_bytes=VMEM_LIMIT, **kw)


def _row_seq_of_tile(i, tile_rows):
    first = i * tile_rows
    return jnp.where(first < N_CTX, 0, 1 + (first - N_CTX) // DEC_SEQ)


def _log_sigmoid(x):
    return -(jnp.maximum(-x, 0.0) + jnp.log1p(jnp.exp(-jnp.abs(x))))


MOD_TN = 1024


def _mod_kernel(c_ref, w_ref, b_ref, o_ref):
    c = c_ref[...]
    s = (c * jax.nn.sigmoid(c)).astype(bf16)
    o_ref[0] = jnp.dot(s, w_ref[0].astype(bf16), preferred_element_type=f32) + b_ref[0]


def _modulation(cond, w_mod, b_mod):
    n_out = 6 * D_MODEL
    return pl.pallas_call(
        _mod_kernel,
        out_shape=jax.ShapeDtypeStruct((DEPTH, N_MODROWS, n_out), f32),
        grid=(DEPTH, n_out // MOD_TN),
        in_specs=[
            pl.BlockSpec((N_MODROWS, D_MODEL), lambda l, j: (0, 0)),
            pl.BlockSpec((1, D_MODEL, MOD_TN), lambda l, j: (l, 0, j)),
            pl.BlockSpec((1, 1, MOD_TN), lambda l, j: (l, 0, j)),
        ],
        out_specs=pl.BlockSpec((1, N_MODROWS, MOD_TN), lambda l, j: (l, 0, j)),
        compiler_params=_cparams(("arbitrary", "arbitrary")),
        name="modulation",
    )(cond, w_mod, b_mod.reshape(DEPTH, 1, n_out))


EMB_TM = 512


def _embed_kernel(xp_ref, xs_ref, pe_ref, o_ref):
    i = pl.program_id(0)

    @pl.when(i < N_CTX // EMB_TM)
    def _():
        o_ref[...] = xp_ref[...]

    @pl.when(i >= N_CTX // EMB_TM)
    def _():
        o_ref[...] = xs_ref[...] + pe_ref[...]


def _embed(x_prompt, x_sample, pe):
    nc = N_CTX // EMB_TM
    per_seq = DEC_SEQ // EMB_TM
    return pl.pallas_call(
        _embed_kernel,
        out_shape=jax.ShapeDtypeStruct((NT, D_MODEL), f32),
        grid=(NT // EMB_TM,),
        in_specs=[
            pl.BlockSpec((EMB_TM, D_MODEL), lambda i: (jnp.minimum(i, nc - 1), 0)),
            pl.BlockSpec((EMB_TM, D_MODEL), lambda i: (jnp.maximum(i - nc, 0), 0)),
            pl.BlockSpec((EMB_TM, D_MODEL), lambda i: (jnp.maximum(i - nc, 0) % per_seq, 0)),
        ],
        out_specs=pl.BlockSpec((EMB_TM, D_MODEL), lambda i: (i, 0)),
        compiler_params=_cparams(("arbitrary",)),
        name="embed",
    )(x_prompt.reshape(N_CTX, D_MODEL), x_sample.reshape(N_LAT, D_MODEL), pe)


def _grid_pos_embed():
    rows = DEC_SEQ // GRID_W
    rr, cc = jnp.meshgrid(jnp.arange(rows, dtype=f32), jnp.arange(GRID_W, dtype=f32), indexing="ij")
    quarter = D_MODEL // 4
    omega = 1.0 / (POS_BASE ** (jnp.arange(quarter, dtype=f32) / quarter))

    def emb(pos):
        ang = pos[:, None] * omega[None, :]
        return jnp.concatenate([jnp.sin(ang), jnp.cos(ang)], axis=-1)

    return jnp.concatenate([emb(rr.reshape(-1)), emb(cc.reshape(-1))], axis=-1)


def _pre_kernel(x_ref, mod_ref, g_ref, w_ref, wglr_ref, wg2_ref, bg_ref, proj_ref, logg_ref, h_ref):
    @pl.when(pl.program_id(1) == 0)
    def _():
        x = x_ref[...]
        y = x * lax.rsqrt(jnp.mean(x * x, axis=-1, keepdims=True) + EPS) * g_ref[...]
        h = y * (1.0 + mod_ref[0, 1:2, :]) + mod_ref[0, 0:1, :]
        hb = h.astype(bf16)
        h_ref[...] = hb
        glr = jnp.dot(hb, wglr_ref[...], preferred_element_type=f32)
        gate = jnp.dot(glr, wg2_ref[...], precision=HIGHEST, preferred_element_type=f32) + bg_ref[...]
        logg_ref[...] = _log_sigmoid(gate) * (1.0 / GLA_GATE_NORM)

    proj_ref[...] = jnp.dot(h_ref[...], w_ref[...], preferred_element_type=f32).astype(bf16)


def _pre(x, mod_l, norm1, w_main, w_glr, w_g2bd, b_g):
    return pl.pallas_call(
        _pre_kernel,
        out_shape=(jax.ShapeDtypeStruct((NT, PROJ_COLS), bf16),
                   jax.ShapeDtypeStruct((NT, N_DIRS * GLA_QK), f32)),
        grid=(NT // PRE_TM, PROJ_COLS // PRE_TN),
        in_specs=[
            pl.BlockSpec((PRE_TM, D_MODEL), lambda i, j: (i, 0)),
            pl.BlockSpec((1, 6, D_MODEL), lambda i, j: (_row_seq_of_tile(i, PRE_TM), 0, 0)),
            pl.BlockSpec((1, D_MODEL), lambda i, j: (0, 0)),
            pl.BlockSpec((D_MODEL, PRE_TN), lambda i, j: (0, j)),
            pl.BlockSpec((D_MODEL, N_DIRS * GLA_RANK), lambda i, j: (0, 0)),
            pl.BlockSpec((N_DIRS * GLA_RANK, N_DIRS * GLA_QK), lambda i, j: (0, 0)),
            pl.BlockSpec((1, N_DIRS * GLA_QK), lambda i, j: (0, 0)),
        ],
        out_specs=(pl.BlockSpec((PRE_TM, PRE_TN), lambda i, j: (i, j)),
                   pl.BlockSpec((PRE_TM, N_DIRS * GLA_QK), lambda i, j: (i, 0))),
        scratch_shapes=[pltpu.VMEM((PRE_TM, D_MODEL), bf16)],
        compiler_params=_cparams(("arbitrary", "arbitrary")),
        name="pre",
    )(x, mod_l, norm1, w_main, w_glr, w_g2bd, b_g)


def _s5_weights(lam_re, lam_im, log_dt, b_re, b_im, c_re, c_im, d):
    T, G, P, H = S5_T, S5_GROUPS, S5_STATE, S5_GROUP
    dt = jnp.exp(log_dt)[..., None]
    z_re, z_im = lam_re * dt, lam_im * dt
    mag = jnp.exp(z_re)
    a_re, a_im = mag * jnp.cos(z_im), mag * jnp.sin(z_im)
    den = lam_re * lam_re + lam_im * lam_im
    n_re, n_im = a_re - 1.0, a_im
    k_re = (n_re * lam_re + n_im * lam_im) / den
    k_im = (n_im * lam_re - n_re * lam_im) / den
    bb_re = k_re[..., None] * b_re - k_im[..., None] * b_im
    bb_im = k_re[..., None] * b_im + k_im[..., None] * b_re
    n = jnp.arange(T + 1, dtype=f32)[:, None, None, None]
    pmag = jnp.exp(n * z_re)
    pw_re, pw_im = pmag * jnp.cos(n * z_im), pmag * jnp.sin(n * z_im)
    abb_re = pw_re[..., None] * bb_re - pw_im[..., None] * bb_im
    abb_im = pw_re[..., None] * bb_im + pw_im[..., None] * bb_re
    kern = (jnp.einsum("dgop,ndgpi->ndgoi", c_re, abb_re[:T], precision=HIGHEST)
            - jnp.einsum("dgop,ndgpi->ndgoi", c_im, abb_im[:T], precision=HIGHEST))

    s_idx = jnp.arange(T)[:, None]
    t_idx = jnp.arange(T)[None, :]
    lag_f = t_idx - s_idx
    kf = jnp.where((lag_f >= 0)[:, :, None, None, None], kern[jnp.clip(lag_f, 0, T - 1), 0], 0.0)
    kb = jnp.where((lag_f <= 0)[:, :, None, None, None], kern[jnp.clip(-lag_f, 0, T - 1), 1], 0.0)
    eye_t = jnp.eye(T, dtype=f32)[:, :, None, None, None]
    eye_h = jnp.eye(H, dtype=f32)[None, None, None, :, :]
    dterm = eye_t * eye_h * d[None, None, :, :, None]
    toeplitz = jnp.transpose(kf + kb + dterm, (2, 0, 4, 1, 3)).reshape(G, T * H, T * H)

    def state_cols(powers, dirn):
        re = jnp.transpose(abb_re[powers, dirn], (1, 0, 3, 2)).reshape(G, T * H, P)
        im = jnp.transpose(abb_im[powers, dirn], (1, 0, 3, 2)).reshape(G, T * H, P)
        return [re, im, im, re]

    w_state = jnp.concatenate(state_cols(T - 1 - jnp.arange(T), 0) + state_cols(jnp.arange(T), 1), axis=-1)

    def carry_rows(powers, dirn):
        pr = pw_re[powers, dirn]
        pi = pw_im[powers, dirn]
        cr, ci = c_re[dirn], c_im[dirn]
        on_re = cr[None] * pr[:, :, None, :] - ci[None] * pi[:, :, None, :]
        on_im = -cr[None] * pi[:, :, None, :] - ci[None] * pr[:, :, None, :]
        to_cols = lambda m: jnp.transpose(m, (1, 3, 0, 2)).reshape(G, P, T * H)
        return [to_cols(on_re), to_cols(on_im)]

    w_carry = jnp.concatenate(carry_rows(1 + jnp.arange(T), 0) + carry_rows(T - jnp.arange(T), 1), axis=1)

    def coef_rows(dirn):
        ar, ai = pw_re[T, dirn], pw_im[T, dirn]
        return [jnp.concatenate([ar, ar], -1), jnp.concatenate([-ai, ai], -1), jnp.concatenate([ai, -ai], -1)]

    zero = jnp.zeros((G, 2 * P), f32)
    coef = jnp.stack(coef_rows(0) + coef_rows(1) + [zero, zero], axis=1)
    return toeplitz.astype(bf16), w_state.astype(bf16), w_carry.astype(bf16), coef


def _s5_kernel(u_ref, tz_ref, ws_ref, wc_ref, coef_ref, h0_ref, y_ref, fin_ref, s_ref, hin_ref):
    u = u_ref[0]
    s_ref[...] = jnp.dot(u, ws_ref[0], preferred_element_type=f32)
    co = coef_ref[0]
    a1f, a2f, a2sf = co[0:1], co[1:2], co[2:3]
    a1b, a2b, a2sb = co[3:4], co[4:5], co[5:6]
    P2 = 2 * S5_STATE

    def step(rf, rb, n_rows, carry):
        hf, hfs, hb, hbs = carry
        hin_ref[pl.ds(rf, n_rows), 0:P2] = hf
        hin_ref[pl.ds(rb, n_rows), P2:2 * P2] = hb
        sf = s_ref[pl.ds(rf, n_rows), 0:P2]
        sfs = s_ref[pl.ds(rf, n_rows), P2:2 * P2]
        sb = s_ref[pl.ds(rb, n_rows), 2 * P2:3 * P2]
        sbs = s_ref[pl.ds(rb, n_rows), 3 * P2:4 * P2]
        return (a1f * hf + a2f * hfs + sf, a1f * hfs + a2sf * hf + sfs,
                a1b * hb + a2b * hbs + sb, a1b * hbs + a2sb * hb + sbs)

    zero = jnp.zeros((BATCH, P2), f32)
    carry = (zero, zero, zero, zero)
    for c in range(S5_CTX_CHUNKS):
        carry = step(c * BATCH, (S5_CTX_CHUNKS - 1 - c) * BATCH, BATCH, carry)
    fin_ref[0] = jnp.concatenate([carry[0], carry[2]], axis=1)

    h0 = h0_ref[0]
    lat0 = S5_CTX_CHUNKS * BATCH

    carry = (h0[:, 0:P2], h0[:, P2:2 * P2], h0[:, 2 * P2:3 * P2], h0[:, 3 * P2:4 * P2])
    for c in range(S5_LAT_CHUNKS):
        carry = step(lat0 + c * DEC_BATCH, lat0 + (S5_LAT_CHUNKS - 1 - c) * DEC_BATCH, DEC_BATCH, carry)

    y = jnp.dot(u, tz_ref[0], preferred_element_type=f32)
    y = y + jnp.dot(hin_ref[...].astype(bf16), wc_ref[0], preferred_element_type=f32)
    y_ref[0] = y.astype(bf16)


def _s5_scan(u_g, toeplitz, w_state, w_carry, coef, h0):
    G, TH, P2 = S5_GROUPS, S5_T * S5_GROUP, 2 * S5_STATE
    return pl.pallas_call(
        _s5_kernel,
        out_shape=(jax.ShapeDtypeStruct((G, S5_ROWS, TH), bf16),
                   jax.ShapeDtypeStruct((G, BATCH, 2 * P2), f32)),
        grid=(G,),
        in_specs=[
            pl.BlockSpec((1, S5_ROWS, TH), lambda g: (g, 0, 0)),
            pl.BlockSpec((1, TH, TH), lambda g: (g, 0, 0)),
            pl.BlockSpec((1, TH, 4 * P2), lambda g: (g, 0, 0)),
            pl.BlockSpec((1, 2 * P2, TH), lambda g: (g, 0, 0)),
            pl.BlockSpec((1, 8, P2), lambda g: (g, 0, 0)),
            pl.BlockSpec((1, DEC_BATCH, 4 * P2), lambda g: (g, 0, 0)),
        ],
        out_specs=(pl.BlockSpec((1, S5_ROWS, TH), lambda g: (g, 0, 0)),
                   pl.BlockSpec((1, BATCH, 2 * P2), lambda g: (g, 0, 0))),
        scratch_shapes=[pltpu.VMEM((S5_ROWS, 4 * P2), f32), pltpu.VMEM((S5_ROWS, 2 * P2), f32)],
        compiler_params=_cparams(("arbitrary",)),
        name="s5_scan",
    )(u_g, toeplitz, w_state, w_carry, coef, h0)


def _s5_to_groups(u):
    G, H, T = S5_GROUPS, S5_GROUP, S5_T
    ctx = u[:N_CTX].reshape(BATCH, S5_CTX_CHUNKS, T, G, H)
    lat = u[N_CTX:].reshape(DEC_BATCH, S5_LAT_CHUNKS, T, G, H)
    ctx = jnp.transpose(ctx, (3, 1, 0, 2, 4)).reshape(G, S5_CTX_CHUNKS * BATCH, T * H)
    lat = jnp.transpose(lat, (3, 1, 0, 2, 4)).reshape(G, S5_LAT_CHUNKS * DEC_BATCH, T * H)
    return jnp.concatenate([ctx, lat], axis=1)


def _s5_from_groups(y):
    G, H, T = S5_GROUPS, S5_GROUP, S5_T
    nc = S5_CTX_CHUNKS * BATCH
    ctx = y[:, :nc].reshape(G, S5_CTX_CHUNKS, BATCH, T, H)
    lat = y[:, nc:].reshape(G, S5_LAT_CHUNKS, DEC_BATCH, T, H)
    ctx = jnp.transpose(ctx, (2, 1, 3, 0, 4)).reshape(N_CTX, G * H)
    lat = jnp.transpose(lat, (2, 1, 3, 0, 4)).reshape(N_LAT, G * H)
    return jnp.concatenate([ctx, lat], axis=0)


def _s5_init_rows(s_re, s_im):
    parts = [s_re[:, 0], s_im[:, 0], s_im[:, 0], s_re[:, 0], s_re[:, 1], s_im[:, 1], s_im[:, 1], s_re[:, 1]]
    return jnp.transpose(jnp.concatenate(parts, axis=-1), (1, 0, 2))


def _gla_seq_of_chunk(cj):
    per_ctx = SEQ // GLA_C
    per_lat = DEC_SEQ // GLA_C
    return jnp.where(cj < GLA_CTX_CHUNKS, cj // per_ctx, BATCH + (cj - GLA_CTX_CHUNKS) // per_lat)


def _gla_kernel(q_ref, k_ref, v_ref, lg_ref, s0_ref, o_ref, st_ref, a_ref, *, reverse):
    j = pl.program_id(0)
    cj = (GLA_CHUNKS - 1 - j) if reverse else j
    per_ctx = SEQ // GLA_C
    per_lat = DEC_SEQ // GLA_C
    pos = jnp.where(cj < GLA_CTX_CHUNKS, cj % per_ctx, (cj - GLA_CTX_CHUNKS) % per_lat)
    n_in_seq = jnp.where(cj < GLA_CTX_CHUNKS, per_ctx, per_lat)
    is_first = (pos == n_in_seq - 1) if reverse else (pos == 0)

    @pl.when(is_first)
    def _():
        st_ref[...] = s0_ref[...]

    C, SUB = GLA_C, GLA_SUB
    row = lax.broadcasted_iota(i32, (C, C), 0)
    col = lax.broadcasted_iota(i32, (C, C), 1)
    if reverse:
        m_cum = (col >= row)
        m_ref = (col >= (row // SUB + 1) * SUB)
    else:
        m_cum = (col <= row)
        m_ref = (col < (row // SUB) * SUB)
    m_both = jnp.concatenate([m_cum.astype(f32), m_ref.astype(f32)], axis=0)
    lg_all = lg_ref[...]
    sums = jnp.dot(m_both, lg_all, precision=HIGHEST, preferred_element_type=f32)
    scale = GLA_DK ** -0.5
    a_ref[...] = jnp.zeros((C, C), f32)

    for h in range(GLA_HEADS):
        ks = slice(h * GLA_DK, (h + 1) * GLA_DK)
        vs = slice(h * GLA_DV, (h + 1) * GLA_DV)
        lg = lg_all[:, ks]
        cum = sums[0:C, ks]
        ref = sums[C:2 * C, ks]
        tot = jnp.sum(lg, axis=0, keepdims=True)
        q = q_ref[:, ks].astype(f32) * scale
        k = k_ref[:, ks].astype(f32)
        v = v_ref[:, vs]
        state = st_ref[0, h]

        q_in = (q * jnp.exp(cum)).astype(bf16)
        o = jnp.dot(q_in, state.astype(bf16), preferred_element_type=f32)
        k_st = k * jnp.exp(tot - cum)
        tot_col = jnp.sum(lg.T, axis=1, keepdims=True)
        st_ref[0, h] = jnp.exp(tot_col) * state + jnp.dot(k_st.T.astype(bf16), v, preferred_element_type=f32)

        qt = (q * jnp.exp(cum - ref)).astype(bf16)
        for blk in range(GLA_NSUB):
            rows = slice(blk * SUB, (blk + 1) * SUB)
            cols = slice((blk + 1) * SUB, C) if reverse else slice(0, blk * SUB)
            if cols.start == cols.stop:
                continue
            kt = (k[cols] * jnp.exp(ref[blk * SUB:blk * SUB + 1] - cum[cols])).astype(bf16)
            a_ref[rows, cols] = lax.dot_general(qt[rows], kt, (((1,), (1,)), ((), ())), preferred_element_type=f32)

        diag = jnp.zeros((C, C), f32)
        for lag in range(SUB):
            shift = (C - lag) % C if reverse else lag
            k_sh = k if lag == 0 else pltpu.roll(k, shift, axis=0)
            c_sh = cum if lag == 0 else pltpu.roll(cum, shift, axis=0)
            z = q * k_sh * jnp.exp(jnp.minimum(cum - c_sh, 0.0))
            r = jnp.sum(z, axis=1, keepdims=True)
            if reverse:
                hit = (col == row + lag) & (row % SUB + lag < SUB)
            else:
                hit = (col == row - lag) & (row % SUB >= lag)
            diag = diag + jnp.where(hit, r, 0.0)
        scores = (a_ref[...] + diag).astype(bf16)
        o_ref[:, vs] = o + jnp.dot(scores, v, preferred_element_type=f32)


def _gla_direction(proj, logg, s0, reverse):
    dirn = 1 if reverse else 0
    cidx = (lambda j: GLA_CHUNKS - 1 - j) if reverse else (lambda j: j)
    q_blk = (S5_WIDTH) // GLA_QK
    v_blk = (S5_WIDTH + 2 * GLA_QK) // GLA_V
    return pl.pallas_call(
        functools.partial(_gla_kernel, reverse=reverse),
        out_shape=(jax.ShapeDtypeStruct((NT, GLA_V), f32),
                   jax.ShapeDtypeStruct((N_SEQS, GLA_HEADS, GLA_DK, GLA_DV), f32)),
        grid=(GLA_CHUNKS,),
        in_specs=[
            pl.BlockSpec((GLA_C, GLA_QK), lambda j: (cidx(j), q_blk)),
            pl.BlockSpec((GLA_C, GLA_QK), lambda j: (cidx(j), q_blk + 1)),
            pl.BlockSpec((GLA_C, GLA_V), lambda j: (cidx(j), v_blk)),
            pl.BlockSpec((GLA_C, GLA_QK), lambda j: (cidx(j), dirn)),
            pl.BlockSpec((1, GLA_HEADS, GLA_DK, GLA_DV), lambda j: (_gla_seq_of_chunk(cidx(j)), 0, 0, 0)),
        ],
        out_specs=(pl.BlockSpec((GLA_C, GLA_V), lambda j: (cidx(j), 0)),
                   pl.BlockSpec((1, GLA_HEADS, GLA_DK, GLA_DV), lambda j: (_gla_seq_of_chunk(cidx(j)), 0, 0, 0))),
        scratch_shapes=[pltpu.VMEM((GLA_C, GLA_C), f32)],
        compiler_params=_cparams(("arbitrary",)),
        name="gla_bwd" if reverse else "gla_fwd",
    )(proj, proj, proj, logg, s0)


def _post_kernel(y_ref, of_ref, ob_ref, r_ref, ga_ref, gb_ref, x_ref, mod_ref,
                 wglu_ref, bglu_ref, ws5_ref, gn_ref, wgla_ref, wout_ref, n2_ref, wr_ref, br_ref,
                 x1_ref, h2_ref, gate_ref, idx_ref):
    g = jax.nn.gelu(y_ref[...].astype(f32))
    glu = jnp.dot(g.astype(bf16), wglu_ref[...], preferred_element_type=f32) + bglu_ref[...]
    ya = g * jax.nn.sigmoid(glu)

    o = of_ref[...] + ob_ref[...]
    heads = []
    for h in range(GLA_HEADS):
        oh = o[:, h * GLA_DV:(h + 1) * GLA_DV]
        heads.append(oh * lax.rsqrt(jnp.mean(oh * oh, axis=-1, keepdims=True) + EPS))
    r = r_ref[...].astype(f32)
    yb = jnp.concatenate(heads, axis=1) * gn_ref[...] * (r * jax.nn.sigmoid(r))

    merged = (jax.nn.sigmoid(ga_ref[...].astype(f32)) * jnp.dot(ya.astype(bf16), ws5_ref[...], preferred_element_type=f32)
              + jax.nn.sigmoid(gb_ref[...].astype(f32)) * jnp.dot(yb.astype(bf16), wgla_ref[...], preferred_element_type=f32))
    x1 = x_ref[...] + mod_ref[0, 2:3, :] * jnp.dot(merged.astype(bf16), wout_ref[...], preferred_element_type=f32)
    x1_ref[...] = x1

    y2 = x1 * lax.rsqrt(jnp.mean(x1 * x1, axis=-1, keepdims=True) + EPS) * n2_ref[...]
    h2 = y2 * (1.0 + mod_ref[0, 4:5, :]) + mod_ref[0, 3:4, :]
    h2_ref[...] = h2

    logits = jnp.dot(h2, wr_ref[...], precision=HIGHEST, preferred_element_type=f32) + br_ref[...]
    lane = lax.broadcasted_iota(i32, logits.shape, 1)
    lane_f = lane.astype(f32)
    vals = logits
    top_v, top_i = [], []
    for _ in range(TOP_K):
        m = jnp.max(vals, axis=-1, keepdims=True)
        idx = jnp.min(jnp.where(vals == m, lane_f, float(LANES)), axis=-1, keepdims=True).astype(i32)
        top_v.append(m)
        top_i.append(idx)
        vals = jnp.where(lane == idx, -jnp.inf, vals)
    ex = [jnp.exp(v - top_v[0]) for v in top_v]
    inv = 1.0 / (ex[0] + ex[1] + ex[2] + ex[3])
    gates = jnp.zeros(logits.shape, f32)
    idxs = jnp.zeros(logits.shape, i32)
    for kk in range(TOP_K):
        gates = jnp.where(lane == kk, ex[kk] * inv, gates)
        idxs = jnp.where(lane == kk, top_i[kk], idxs)
    gate_ref[...] = gates
    idx_ref[...] = idxs


def _post(y_s5, o_f, o_b, proj, x, mod_l, wglu, bglu, ws5, gnorm, wgla, wout, norm2, wr, br):
    const = lambda shape: pl.BlockSpec(shape, lambda i: (0,) * len(shape), pipeline_mode=pl.Buffered(1))
    r_blk = (S5_WIDTH + 2 * GLA_QK + GLA_V) // GLA_V
    ga_blk = 4096 // D_MODEL
    return pl.pallas_call(
        _post_kernel,
        out_shape=(jax.ShapeDtypeStruct((NT, D_MODEL), f32),
                   jax.ShapeDtypeStruct((NT, D_MODEL), f32),
                   jax.ShapeDtypeStruct((NT, LANES), f32),
                   jax.ShapeDtypeStruct((NT, LANES), i32)),
        grid=(NT // POST_TM,),
        in_specs=[
            pl.BlockSpec((POST_TM, S5_WIDTH), lambda i: (i, 0)),
            pl.BlockSpec((POST_TM, GLA_V), lambda i: (i, 0)),
            pl.BlockSpec((POST_TM, GLA_V), lambda i: (i, 0)),
            pl.BlockSpec((POST_TM, GLA_V), lambda i: (i, r_blk)),
            pl.BlockSpec((POST_TM, D_MODEL), lambda i: (i, ga_blk)),
            pl.BlockSpec((POST_TM, D_MODEL), lambda i: (i, ga_blk + 1)),
            pl.BlockSpec((POST_TM, D_MODEL), lambda i: (i, 0)),
            pl.BlockSpec((1, 6, D_MODEL), lambda i: (_row_seq_of_tile(i, POST_TM), 0, 0)),
            const((S5_WIDTH, S5_WIDTH)), const((1, S5_WIDTH)), const((S5_WIDTH, D_MODEL)),
            const((1, GLA_V)), const((GLA_V, D_MODEL)), const((D_MODEL, D_MODEL)), const((1, D_MODEL)),
            const((D_MODEL, LANES)), const((1, LANES)),
        ],
        out_specs=(pl.BlockSpec((POST_TM, D_MODEL), lambda i: (i, 0)),
                   pl.BlockSpec((POST_TM, D_MODEL), lambda i: (i, 0)),
                   pl.BlockSpec((POST_TM, LANES), lambda i: (i, 0)),
                   pl.BlockSpec((POST_TM, LANES), lambda i: (i, 0))),
        compiler_params=_cparams(("arbitrary",)),
        name="post",
    )(y_s5, o_f, o_b, proj, proj, proj, x, mod_l, wglu, bglu, ws5, gnorm, wgla, wout, norm2, wr, br)


def _routing(top_idx):
    m = NT * TOP_K
    e_flat = top_idx.reshape(-1)
    experts = jnp.arange(N_EXPERTS, dtype=i32)
    onehot = (e_flat[:, None] == experts[None, :]).astype(i32)
    csum = jnp.cumsum(onehot, axis=0)
    counts = csum[-1]
    starts = jnp.cumsum(counts) - counts
    nblk = (counts + MOE_BM - 1) // MOE_BM
    blk_end = jnp.cumsum(nblk)
    blk_start = blk_end - nblk
    dest = jnp.sum(onehot * (blk_start[None, :] * MOE_BM + csum - 1), axis=1).astype(i32)

    order = jnp.argsort(e_flat, stable=True).astype(i32)
    p = jnp.arange(MOE_ROWS, dtype=i32)
    e_p = jnp.minimum(jnp.sum((blk_end[None, :] <= (p // MOE_BM)[:, None]).astype(i32), axis=1), N_EXPERTS - 1)
    oh_p = (e_p[:, None] == experts[None, :]).astype(i32)
    pick = lambda v: jnp.sum(oh_p * v[None, :], axis=1)
    idx_in = p - pick(blk_start) * MOE_BM
    valid = (p < blk_end[-1] * MOE_BM) & (idx_in < pick(counts))
    src = order[jnp.clip(pick(starts) + idx_in, 0, m - 1)] // TOP_K
    row_tok = jnp.where(valid, src, 0).astype(i32)

    n_units = (nblk + MOE_RB - 1) // MOE_RB
    unit_end = jnp.cumsum(n_units)
    unit_start = unit_end - n_units
    total_units = unit_end[-1]
    u = jnp.arange(MOE_UNITS, dtype=i32)
    used = u < total_units
    e_of_u = jnp.minimum(jnp.sum((unit_end[None, :] <= u[:, None]).astype(i32), axis=1), N_EXPERTS - 1)
    local = u - unit_start[e_of_u]
    u_blk0 = blk_start[e_of_u] + local * MOE_RB
    u_nblk = jnp.minimum(MOE_RB, nblk[e_of_u] - local * MOE_RB)
    last_e = e_of_u[jnp.maximum(total_units - 1, 0)]
    u_exp = jnp.where(used, e_of_u, last_e).astype(i32)
    u_blk0 = jnp.where(used, u_blk0, 0).astype(i32)
    u_nblk = jnp.where(used, u_nblk, 0).astype(i32)
    return row_tok, dest, u_exp, u_blk0, u_nblk, blk_end[-1:].astype(i32)


def _gather_kernel(tok_ref, src_ref, o_ref, stage_ref, sem):
    base = pl.program_id(0) * GATHER_GB

    def row_copy(t, r):
        return pltpu.make_async_copy(src_ref.at[pl.ds(t, 1), :], stage_ref.at[pl.ds(r, 1), :], sem)

    def issue(r, c):
        row_copy(tok_ref[base + r], r).start()
        return c

    lax.fori_loop(0, GATHER_GB, issue, 0)

    def wait(r, c):
        row_copy(0, r).wait()
        return c

    lax.fori_loop(0, GATHER_GB, wait, 0)
    o_ref[...] = stage_ref[...].astype(bf16)


def _gather_rows(h2, row_tok):
    return pl.pallas_call(
        _gather_kernel,
        out_shape=jax.ShapeDtypeStruct((MOE_ROWS, D_MODEL), bf16),
        grid_spec=pltpu.PrefetchScalarGridSpec(
            num_scalar_prefetch=1,
            grid=(MOE_ROWS // GATHER_GB,),
            in_specs=[pl.BlockSpec(memory_space=pl.ANY)],
            out_specs=pl.BlockSpec((GATHER_GB, D_MODEL), lambda i, tok: (i, 0)),
            scratch_shapes=[pltpu.VMEM((GATHER_GB, D_MODEL), f32), pltpu.SemaphoreType.DMA(())],
        ),
        compiler_params=_cparams(("arbitrary",)),
        name="moe_gather",
    )(row_tok, h2)


def _deinterleave_matrix():
    half = MOE_TA // 2
    src = jnp.arange(MOE_TA)[:, None]
    dst = jnp.arange(MOE_TA)[None, :]
    return (dst == (src % 2) * half + src // 2).astype(bf16)


def _moe_kernel(uexp_ref, ublk_ref, unb_ref, nused_ref, x_hbm, w1_ref, b1_ref, perm_ref, w2_ref, b2_ref,
                y_hbm, x_buf, act_buf, w_buf, y_buf, zero_buf, sem_in, sem_out):
    u = pl.program_id(0)
    j = pl.program_id(1)
    nb = unb_ref[u]
    row0 = ublk_ref[u] * MOE_BM

    def in_copy(r):
        rows = pl.ds(pl.multiple_of(r * MOE_BM, MOE_BM), MOE_BM)
        src = x_hbm.at[pl.ds(pl.multiple_of(row0 + r * MOE_BM, MOE_BM), MOE_BM), :]
        return pltpu.make_async_copy(src, x_buf.at[rows, :], sem_in)

    @pl.when((j == 0) & (nb > 0))
    def _():
        lax.fori_loop(0, nb, lambda r, c: (in_copy(r).start(), c)[1], 0)
        lax.fori_loop(0, nb, lambda r, c: (in_copy(r).wait(), c)[1], 0)

    @pl.when((j < MOE_J1) & (nb > 0))
    def _():
        w_buf[...] = w1_ref[0].astype(bf16)
        half = MOE_TA // 2

        def body(r, c):
            rows = pl.ds(pl.multiple_of(r * MOE_BM, MOE_BM), MOE_BM)
            h = jnp.dot(x_buf[rows, :], w_buf[...], preferred_element_type=f32) + b1_ref[0]
            hb = h.astype(bf16)
            parts = []
            for s in range(MOE_TN // MOE_TA):
                hp = jnp.dot(hb[:, s * MOE_TA:(s + 1) * MOE_TA], perm_ref[...], preferred_element_type=f32)
                hg = jnp.minimum(hp[:, :half], SWIGLU_LIMIT)
                hl = jnp.clip(hp[:, half:], -SWIGLU_LIMIT, SWIGLU_LIMIT)
                parts.append(hg * jax.nn.sigmoid(SWIGLU_ALPHA * hg) * (hl + 1.0))
            act_buf[j, rows, :] = jnp.concatenate(parts, axis=1).astype(bf16)
            return c

        lax.fori_loop(0, nb, body, 0)

    @pl.when((j >= MOE_J1) & (nb > 0))
    def _():
        w_buf[...] = w2_ref[0].astype(bf16)
        col0 = pl.multiple_of((j - MOE_J1) * MOE_TN, MOE_TN)

        def out_copy(r):
            rows = pl.ds(pl.multiple_of(r * MOE_BM, MOE_BM), MOE_BM)
            dst = y_hbm.at[pl.ds(pl.multiple_of(row0 + r * MOE_BM, MOE_BM), MOE_BM), pl.ds(col0, MOE_TN)]
            return pltpu.make_async_copy(y_buf.at[rows, :], dst, sem_out)

        def body(r, c):
            rows = pl.ds(pl.multiple_of(r * MOE_BM, MOE_BM), MOE_BM)
            acc = b2_ref[0] + jnp.zeros((MOE_BM, MOE_TN), f32)
            for jj in range(MOE_J1):
                acc = acc + jnp.dot(act_buf[jj, rows, :], w_buf[jj * MOE_TA:(jj + 1) * MOE_TA, :],
                                    preferred_element_type=f32)
            y_buf[rows, :] = acc
            out_copy(r).start()
            return c

        lax.fori_loop(0, nb, body, 0)
        lax.fori_loop(0, nb, lambda r, c: (out_copy(r).wait(), c)[1], 0)

    @pl.when((u == MOE_UNITS - 1) & (j == MOE_J1 + MOE_J2 - 1))
    def _():
        zero_buf[...] = jnp.zeros(zero_buf.shape, f32)

        def fill_copy(r):
            dst = y_hbm.at[pl.ds(pl.multiple_of(r * MOE_BM, MOE_BM), MOE_BM), :]
            return pltpu.make_async_copy(zero_buf, dst, sem_out)

        lax.fori_loop(nused_ref[0], MOE_BLOCKS, lambda r, c: (fill_copy(r).start(), c)[1], 0)
        lax.fori_loop(nused_ref[0], MOE_BLOCKS, lambda r, c: (fill_copy(r).wait(), c)[1], 0)


def _moe_experts(x_sorted, u_exp, u_blk0, u_nblk, n_used, w1, b1, w2, b2):
    def w1_idx(u, j, uexp, ublk, unb, nused):
        return (uexp[u], 0, jnp.where(unb[u] > 0, jnp.minimum(j, MOE_J1 - 1), MOE_J1 - 1))

    def w2_idx(u, j, uexp, ublk, unb, nused):
        return (uexp[u], 0, jnp.where(unb[u] > 0, jnp.maximum(j - MOE_J1, 0), MOE_J2 - 1))

    return pl.pallas_call(
        _moe_kernel,
        out_shape=jax.ShapeDtypeStruct((MOE_ROWS, D_MODEL), f32),
        grid_spec=pltpu.PrefetchScalarGridSpec(
            num_scalar_prefetch=4,
            grid=(MOE_UNITS, MOE_J1 + MOE_J2),
            in_specs=[
                pl.BlockSpec(memory_space=pl.ANY),
                pl.BlockSpec((1, D_MODEL, MOE_TN), w1_idx),
                pl.BlockSpec((1, 1, MOE_TN), w1_idx),
                pl.BlockSpec((MOE_TA, MOE_TA), lambda u, j, *_: (0, 0)),
                pl.BlockSpec((1, D_MODEL, MOE_TN), w2_idx),
                pl.BlockSpec((1, 1, MOE_TN), w2_idx),
            ],
            out_specs=pl.BlockSpec(memory_space=pl.ANY),
            scratch_shapes=[
                pltpu.VMEM((MOE_RMAX, D_MODEL), bf16),
                pltpu.VMEM((MOE_J1, MOE_RMAX, MOE_TA), bf16),
                pltpu.VMEM((D_MODEL, MOE_TN), bf16),
                pltpu.VMEM((MOE_RMAX, MOE_TN), f32),
                pltpu.VMEM((MOE_BM, D_MODEL), f32),
                pltpu.SemaphoreType.DMA(()),
                pltpu.SemaphoreType.DMA(()),
            ],
        ),
        compiler_params=_cparams(("arbitrary", "arbitrary")),
        name="moe_experts",
    )(u_exp, u_blk0, u_nblk, n_used, x_sorted, w1, b1, _deinterleave_matrix(), w2, b2)


def _combine_kernel(dest_ref, y_hbm, gate_ref, x1_ref, mod_ref, nf_ref, o_ref, y_buf, sem, *, final_norm):
    base = pl.program_id(0) * COMB_TM * TOP_K

    def row_copy(p, kk, r):
        return pltpu.make_async_copy(y_hbm.at[pl.ds(p, 1), :], y_buf.at[kk, pl.ds(r, 1), :], sem)

    def issue(r, c):
        for kk in range(TOP_K):
            row_copy(dest_ref[base + r * TOP_K + kk], kk, r).start()
        return c

    lax.fori_loop(0, COMB_TM, issue, 0)

    def wait(r, c):
        for kk in range(TOP_K):
            row_copy(0, kk, r).wait()
        return c

    lax.fori_loop(0, COMB_TM, wait, 0)

    gates = gate_ref[...]
    acc = jnp.zeros((COMB_TM, D_MODEL), f32)
    for kk in range(TOP_K):
        acc = acc + gates[:, kk:kk + 1] * y_buf[kk]
    x2 = x1_ref[...] + mod_ref[0, 5:6, :] * acc
    if final_norm:
        x2 = x2 * lax.rsqrt(jnp.mean(x2 * x2, axis=-1, keepdims=True) + EPS) * nf_ref[...]
    o_ref[...] = x2


def _combine(y_sorted, dest, gates, x1, mod_l, norm_f, final_norm):
    return pl.pallas_call(
        functools.partial(_combine_kernel, final_norm=final_norm),
        out_shape=jax.ShapeDtypeStruct((NT, D_MODEL), f32),
        grid_spec=pltpu.PrefetchScalarGridSpec(
            num_scalar_prefetch=1,
            grid=(NT // COMB_TM,),
            in_specs=[
                pl.BlockSpec(memory_space=pl.ANY),
                pl.BlockSpec((COMB_TM, LANES), lambda i, d: (i, 0)),
                pl.BlockSpec((COMB_TM, D_MODEL), lambda i, d: (i, 0)),
                pl.BlockSpec((1, 6, D_MODEL), lambda i, d: (_row_seq_of_tile(i, COMB_TM), 0, 0)),
                pl.BlockSpec((1, D_MODEL), lambda i, d: (0, 0)),
            ],
            out_specs=pl.BlockSpec((COMB_TM, D_MODEL), lambda i, d: (i, 0)),
            scratch_shapes=[pltpu.VMEM((TOP_K, COMB_TM, D_MODEL), f32), pltpu.SemaphoreType.DMA(())],
        ),
        compiler_params=_cparams(("arbitrary",)),
        name="moe_combine_final" if final_norm else "moe_combine",
    )(dest, y_sorted, gates, x1, mod_l, norm_f)


def kernel(x_prompt, x_sample, c, state_s5_re, state_s5_im, state_gla, c_ctx, w_mod, b_mod, norm1, w_in, s5_lam_re, s5_lam_im, s5_log_dt, s5_b_re, s5_b_im, s5_c_re, s5_c_im, s5_d, s5_w_glu, s5_b_glu, w_s5_out, gla_w_g2, gla_b_g, gla_norm, w_gla_out, w_out, norm2, w_router, b_router, w_e1, b_e1, w_e2, b_e2, norm_f):
    cond = jnp.concatenate([c_ctx[None], c, jnp.zeros((N_MODROWS - 1 - DEC_BATCH, D_MODEL), f32)], axis=0)
    mod = _modulation(cond, w_mod, b_mod).reshape(DEPTH, N_MODROWS, 6, D_MODEL)
    x = _embed(x_prompt, x_sample, _grid_pos_embed())

    glr0 = S5_WIDTH + 2 * GLA_QK + 2 * GLA_V
    glr1 = glr0 + N_DIRS * GLA_RANK
    s5_re_out, s5_im_out, gla_out = [], [], []
    for l in range(DEPTH):
        mod_l = mod[l]
        w_main = jnp.concatenate([w_in[l, :, :glr0], w_in[l, :, glr1:]], axis=1).astype(bf16)
        w_glr = w_in[l, :, glr0:glr1].astype(bf16)
        zg = jnp.zeros((GLA_RANK, GLA_QK), f32)
        w_g2bd = jnp.concatenate([jnp.concatenate([gla_w_g2[l, 0], zg], axis=1),
                                  jnp.concatenate([zg, gla_w_g2[l, 1]], axis=1)], axis=0)
        b_g = gla_b_g[l].reshape(1, N_DIRS * GLA_QK)
        toeplitz, w_state, w_carry, coef = _s5_weights(
            s5_lam_re[l], s5_lam_im[l], s5_log_dt[l], s5_b_re[l], s5_b_im[l], s5_c_re[l], s5_c_im[l], s5_d[l])
        h0 = _s5_init_rows(state_s5_re[:, l], state_s5_im[:, l])
        zero_state = jnp.zeros((BATCH, GLA_HEADS, GLA_DK, GLA_DV), f32)
        w_r = jnp.concatenate([w_router[l], jnp.zeros((D_MODEL, LANES - N_EXPERTS), f32)], axis=1)
        b_r = jnp.concatenate([b_router[l], jnp.full((LANES - N_EXPERTS,), -jnp.inf, f32)]).reshape(1, LANES)
        b1 = b_e1[l].reshape(N_EXPERTS, 1, 2 * D_MODEL)
        b2 = b_e2[l].reshape(N_EXPERTS, 1, D_MODEL)

        proj, logg = _pre(x, mod_l, norm1[l].reshape(1, D_MODEL), w_main, w_glr, w_g2bd, b_g)
        y_g, s5_fin = _s5_scan(_s5_to_groups(proj[:, :S5_WIDTH]), toeplitz, w_state, w_carry, coef, h0)
        y_s5 = _s5_from_groups(y_g)
        o_f, st_f = _gla_direction(proj, logg, jnp.concatenate([zero_state, state_gla[:, l, 0]], axis=0), False)
        o_b, st_b = _gla_direction(proj, logg, jnp.concatenate([zero_state, state_gla[:, l, 1]], axis=0), True)

        x1, h2, gates, top_idx = _post(
            y_s5, o_f, o_b, proj, x, mod_l,
            s5_w_glu[l].astype(bf16), s5_b_glu[l].reshape(1, S5_WIDTH), w_s5_out[l].astype(bf16),
            jnp.tile(gla_norm[l], GLA_HEADS).reshape(1, GLA_V), w_gla_out[l].astype(bf16), w_out[l].astype(bf16),
            norm2[l].reshape(1, D_MODEL), w_r, b_r)

        row_tok, dest, u_exp, u_blk0, u_nblk, n_used = _routing(top_idx[:, :TOP_K])
        x_sorted = _gather_rows(h2, row_tok)
        y_sorted = _moe_experts(x_sorted, u_exp, u_blk0, u_nblk, n_used, w_e1[l], b1, w_e2[l], b2)
        x = _combine(y_sorted, dest, gates, x1, mod_l, norm_f.reshape(1, D_MODEL), l == DEPTH - 1)

        P = S5_STATE
        fin = jnp.transpose(s5_fin, (1, 0, 2))
        s5_re_out.append(jnp.stack([fin[..., 0:P], fin[..., 2 * P:3 * P]], axis=1))
        s5_im_out.append(jnp.stack([fin[..., P:2 * P], fin[..., 3 * P:4 * P]], axis=1))
        gla_out.append(jnp.stack([st_f[:BATCH], st_b[:BATCH]], axis=1))

    y_prompt = x[:N_CTX].reshape(BATCH, SEQ, D_MODEL)
    y_sample = x[N_CTX:].reshape(DEC_BATCH, DEC_SEQ, D_MODEL)
    return (y_prompt, y_sample, jnp.stack(s5_re_out, axis=1), jnp.stack(s5_im_out, axis=1),
            jnp.stack(gla_out, axis=1))
```

```python
import functools
import math

import jax
import jax.numpy as jnp
from jax import lax
from jax.experimental import pallas as pl
from jax.experimental.pallas import tpu as pltpu

f32 = jnp.float32
bf16 = jnp.bfloat16
i32 = jnp.int32
HIGHEST = lax.Precision.HIGHEST

D_MODEL = 2048
BATCH = 16
SEQ = 256
DEPTH = 2
DEC_BATCH = 2
DEC_SEQ = 2048
GRID_W = 64
N_DIRS = 2
S5_WIDTH = 1024
S5_GROUP = 16
S5_GROUPS = 64
S5_STATE = 64
GLA_HEADS = 4
GLA_DK = 128
GLA_DV = 256
GLA_QK = 512
GLA_V = 1024
GLA_RANK = 16
GLA_GATE_NORM = 16.0
N_EXPERTS = 32
TOP_K = 4
SWIGLU_LIMIT = 7.0
SWIGLU_ALPHA = 1.702
POS_BASE = 10000.0
EPS = 1e-6

N_CTX = BATCH * SEQ
N_LAT = DEC_BATCH * DEC_SEQ
NT = N_CTX + N_LAT
N_SEQS = BATCH + DEC_BATCH
N_MODROWS = 8

LANES = 128
VMEM_LIMIT = 56 * 1024 * 1024

S5_T = 16
S5_ROWS = NT // S5_T
S5_CTX_CHUNKS = SEQ // S5_T
S5_LAT_CHUNKS = DEC_SEQ // S5_T
GLA_C = 128
GLA_SUB = 16
GLA_NSUB = GLA_C // GLA_SUB
GLA_CHUNKS = NT // GLA_C
GLA_CTX_CHUNKS = N_CTX // GLA_C
PRE_TM = 1024
PRE_TN = 1024
PROJ_COLS = 8192
POST_TM = 256
MOE_BM = 128
MOE_ROWS = NT * TOP_K + N_EXPERTS * MOE_BM
MOE_BLOCKS = MOE_ROWS // MOE_BM
MOE_RB = 16
MOE_QB = 4
MOE_RMAX = MOE_RB * MOE_BM
MOE_UNITS = N_EXPERTS + MOE_BLOCKS // MOE_RB
MOE_TN = 512
MOE_TA = MOE_TN // 2
MOE_J1 = 2 * D_MODEL // MOE_TN
MOE_J2 = D_MODEL // MOE_TN
GATHER_GB = 256
GATHER_SPLIT = D_MODEL // LANES
COMB_TM = 128


def _cparams(sem, **kw):
    return pltpu.CompilerParams(dimension_semantics=sem, vmem_limit_bytes=VMEM_LIMIT, **kw)


def _row_seq_of_tile(i, tile_rows):
    first = i * tile_rows
    return jnp.where(first < N_CTX, 0, 1 + (first - N_CTX) // DEC_SEQ)


def _log_sigmoid(x):
    return -(jnp.maximum(-x, 0.0) + jnp.log1p(jnp.exp(-jnp.abs(x))))


MOD_TN = 1024


def _mod_kernel(c_ref, w_ref, b_ref, o_ref):
    c = c_ref[...]
    s = (c * jax.nn.sigmoid(c)).astype(bf16)
    o_ref[0] = jnp.dot(s, w_ref[0].astype(bf16), preferred_element_type=f32) + b_ref[0]


def _modulation(cond, w_mod, b_mod):
    n_out = 6 * D_MODEL
    return pl.pallas_call(
        _mod_kernel,
        out_shape=jax.ShapeDtypeStruct((DEPTH, N_MODROWS, n_out), f32),
        grid=(DEPTH, n_out // MOD_TN),
        in_specs=[
            pl.BlockSpec((N_MODROWS, D_MODEL), lambda l, j: (0, 0)),
            pl.BlockSpec((1, D_MODEL, MOD_TN), lambda l, j: (l, 0, j)),
            pl.BlockSpec((1, 1, MOD_TN), lambda l, j: (l, 0, j)),
        ],
        out_specs=pl.BlockSpec((1, N_MODROWS, MOD_TN), lambda l, j: (l, 0, j)),
        compiler_params=_cparams(("arbitrary", "arbitrary")),
        name="modulation",
    )(cond, w_mod, b_mod.reshape(DEPTH, 1, n_out))


EMB_TM = 512


def _embed_kernel(xp_ref, xs_ref, pe_ref, o_ref):
    i = pl.program_id(0)

    @pl.when(i < N_CTX // EMB_TM)
    def _():
        o_ref[...] = xp_ref[...]

    @pl.when(i >= N_CTX // EMB_TM)
    def _():
        o_ref[...] = xs_ref[...] + pe_ref[...]


def _embed(x_prompt, x_sample, pe):
    nc = N_CTX // EMB_TM
    per_seq = DEC_SEQ // EMB_TM
    return pl.pallas_call(
        _embed_kernel,
        out_shape=jax.ShapeDtypeStruct((NT, D_MODEL), f32),
        grid=(NT // EMB_TM,),
        in_specs=[
            pl.BlockSpec((EMB_TM, D_MODEL), lambda i: (jnp.minimum(i, nc - 1), 0)),
            pl.BlockSpec((EMB_TM, D_MODEL), lambda i: (jnp.maximum(i - nc, 0), 0)),
            pl.BlockSpec((EMB_TM, D_MODEL), lambda i: (jnp.maximum(i - nc, 0) % per_seq, 0)),
        ],
        out_specs=pl.BlockSpec((EMB_TM, D_MODEL), lambda i: (i, 0)),
        compiler_params=_cparams(("arbitrary",)),
        name="embed",
    )(x_prompt.reshape(N_CTX, D_MODEL), x_sample.reshape(N_LAT, D_MODEL), pe)


def _grid_pos_embed():
    rows = DEC_SEQ // GRID_W
    rr, cc = jnp.meshgrid(jnp.arange(rows, dtype=f32), jnp.arange(GRID_W, dtype=f32), indexing="ij")
    quarter = D_MODEL // 4
    omega = 1.0 / (POS_BASE ** (jnp.arange(quarter, dtype=f32) / quarter))

    def emb(pos):
        ang = pos[:, None] * omega[None, :]
        return jnp.concatenate([jnp.sin(ang), jnp.cos(ang)], axis=-1)

    return jnp.concatenate([emb(rr.reshape(-1)), emb(cc.reshape(-1))], axis=-1)


def _pre_kernel(x_ref, mod_ref, g_ref, w_ref, wglr_ref, wg2_ref, bg_ref, proj_ref, logg_ref, h_ref):
    @pl.when(pl.program_id(1) == 0)
    def _():
        x = x_ref[...]
        y = x * lax.rsqrt(jnp.mean(x * x, axis=-1, keepdims=True) + EPS) * g_ref[...]
        h = y * (1.0 + mod_ref[0, 1:2, :]) + mod_ref[0, 0:1, :]
        hb = h.astype(bf16)
        h_ref[...] = hb
        glr = jnp.dot(hb, wglr_ref[...], preferred_element_type=f32)
        gate = jnp.dot(glr, wg2_ref[...], precision=HIGHEST, preferred_element_type=f32) + bg_ref[...]
        logg_ref[...] = _log_sigmoid(gate) * (1.0 / GLA_GATE_NORM)

    proj_ref[...] = jnp.dot(h_ref[...], w_ref[...], preferred_element_type=f32).astype(bf16)


def _pre(x, mod_l, norm1, w_main, w_glr, w_g2bd, b_g):
    return pl.pallas_call(
        _pre_kernel,
        out_shape=(jax.ShapeDtypeStruct((NT, PROJ_COLS), bf16),
                   jax.ShapeDtypeStruct((NT, N_DIRS * GLA_QK), f32)),
        grid=(NT // PRE_TM, PROJ_COLS // PRE_TN),
        in_specs=[
            pl.BlockSpec((PRE_TM, D_MODEL), lambda i, j: (i, 0)),
            pl.BlockSpec((1, 6, D_MODEL), lambda i, j: (_row_seq_of_tile(i, PRE_TM), 0, 0)),
            pl.BlockSpec((1, D_MODEL), lambda i, j: (0, 0)),
            pl.BlockSpec((D_MODEL, PRE_TN), lambda i, j: (0, j)),
            pl.BlockSpec((D_MODEL, N_DIRS * GLA_RANK), lambda i, j: (0, 0)),
            pl.BlockSpec((N_DIRS * GLA_RANK, N_DIRS * GLA_QK), lambda i, j: (0, 0)),
            pl.BlockSpec((1, N_DIRS * GLA_QK), lambda i, j: (0, 0)),
        ],
        out_specs=(pl.BlockSpec((PRE_TM, PRE_TN), lambda i, j: (i, j)),
                   pl.BlockSpec((PRE_TM, N_DIRS * GLA_QK), lambda i, j: (i, 0))),
        scratch_shapes=[pltpu.VMEM((PRE_TM, D_MODEL), bf16)],
        compiler_params=_cparams(("arbitrary", "arbitrary")),
        name="pre",
    )(x, mod_l, norm1, w_main, w_glr, w_g2bd, b_g)


def _s5_weights(lam_re, lam_im, log_dt, b_re, b_im, c_re, c_im, d):
    T, G, P, H = S5_T, S5_GROUPS, S5_STATE, S5_GROUP
    dt = jnp.exp(log_dt)[..., None]
    z_re, z_im = lam_re * dt, lam_im * dt
    mag = jnp.exp(z_re)
    a_re, a_im = mag * jnp.cos(z_im), mag * jnp.sin(z_im)
    den = lam_re * lam_re + lam_im * lam_im
    n_re, n_im = a_re - 1.0, a_im
    k_re = (n_re * lam_re + n_im * lam_im) / den
    k_im = (n_im * lam_re - n_re * lam_im) / den
    bb_re = k_re[..., None] * b_re - k_im[..., None] * b_im
    bb_im = k_re[..., None] * b_im + k_im[..., None] * b_re
    n = jnp.arange(T + 1, dtype=f32)[:, None, None, None]
    pmag = jnp.exp(n * z_re)
    pw_re, pw_im = pmag * jnp.cos(n * z_im), pmag * jnp.sin(n * z_im)
    abb_re = pw_re[..., None] * bb_re - pw_im[..., None] * bb_im
    abb_im = pw_re[..., None] * bb_im + pw_im[..., None] * bb_re
    kern = (jnp.einsum("dgop,ndgpi->ndgoi", c_re, abb_re[:T], precision=HIGHEST)
            - jnp.einsum("dgop,ndgpi->ndgoi", c_im, abb_im[:T], precision=HIGHEST))

    s_idx = jnp.arange(T)[:, None]
    t_idx = jnp.arange(T)[None, :]
    lag_f = t_idx - s_idx
    kf = jnp.where((lag_f >= 0)[:, :, None, None, None], kern[jnp.clip(lag_f, 0, T - 1), 0], 0.0)
    kb = jnp.where((lag_f <= 0)[:, :, None, None, None], kern[jnp.clip(-lag_f, 0, T - 1), 1], 0.0)
    eye_t = jnp.eye(T, dtype=f32)[:, :, None, None, None]
    eye_h = jnp.eye(H, dtype=f32)[None, None, None, :, :]
    dterm = eye_t * eye_h * d[None, None, :, :, None]
    toeplitz = jnp.transpose(kf + kb + dterm, (2, 0, 4, 1, 3)).reshape(G, T * H, T * H)

    def state_cols(powers, dirn):
        re = jnp.transpose(abb_re[powers, dirn], (1, 0, 3, 2)).reshape(G, T * H, P)
        im = jnp.transpose(abb_im[powers, dirn], (1, 0, 3, 2)).reshape(G, T * H, P)
        return [re, im, im, re]

    w_state = jnp.concatenate(state_cols(T - 1 - jnp.arange(T), 0) + state_cols(jnp.arange(T), 1), axis=-1)

    def carry_rows(powers, dirn):
        pr = pw_re[powers, dirn]
        pi = pw_im[powers, dirn]
        cr, ci = c_re[dirn], c_im[dirn]
        on_re = cr[None] * pr[:, :, None, :] - ci[None] * pi[:, :, None, :]
        on_im = -cr[None] * pi[:, :, None, :] - ci[None] * pr[:, :, None, :]
        to_cols = lambda m: jnp.transpose(m, (1, 3, 0, 2)).reshape(G, P, T * H)
        return [to_cols(on_re), to_cols(on_im)]

    w_carry = jnp.concatenate(carry_rows(1 + jnp.arange(T), 0) + carry_rows(T - jnp.arange(T), 1), axis=1)

    def coef_rows(dirn):
        ar, ai = pw_re[T, dirn], pw_im[T, dirn]
        return [jnp.concatenate([ar, ar], -1), jnp.concatenate([-ai, ai], -1), jnp.concatenate([ai, -ai], -1)]

    zero = jnp.zeros((G, 2 * P), f32)
    coef = jnp.stack(coef_rows(0) + coef_rows(1) + [zero, zero], axis=1)
    return toeplitz.astype(bf16), w_state.astype(bf16), w_carry.astype(bf16), coef


def _s5_kernel(u_ref, tz_ref, ws_ref, wc_ref, coef_ref, h0_ref, y_ref, fin_ref, s_ref, hin_ref):
    u = u_ref[0]
    s_ref[...] = jnp.dot(u, ws_ref[0], preferred_element_type=f32)
    co = coef_ref[0]
    a1f, a2f, a2sf = co[0:1], co[1:2], co[2:3]
    a1b, a2b, a2sb = co[3:4], co[4:5], co[5:6]
    P2 = 2 * S5_STATE

    def step(rf, rb, n_rows, carry):
        hf, hfs, hb, hbs = carry
        hin_ref[pl.ds(rf, n_rows), 0:P2] = hf
        hin_ref[pl.ds(rb, n_rows), P2:2 * P2] = hb
        sf = s_ref[pl.ds(rf, n_rows), 0:P2]
        sfs = s_ref[pl.ds(rf, n_rows), P2:2 * P2]
        sb = s_ref[pl.ds(rb, n_rows), 2 * P2:3 * P2]
        sbs = s_ref[pl.ds(rb, n_rows), 3 * P2:4 * P2]
        return (a1f * hf + a2f * hfs + sf, a1f * hfs + a2sf * hf + sfs,
                a1b * hb + a2b * hbs + sb, a1b * hbs + a2sb * hb + sbs)

    zero = jnp.zeros((BATCH, P2), f32)
    carry = (zero, zero, zero, zero)
    for c in range(S5_CTX_CHUNKS):
        carry = step(c * BATCH, (S5_CTX_CHUNKS - 1 - c) * BATCH, BATCH, carry)
    fin_ref[0] = jnp.concatenate([carry[0], carry[2]], axis=1)

    h0 = h0_ref[0]
    lat0 = S5_CTX_CHUNKS * BATCH

    carry = (h0[:, 0:P2], h0[:, P2:2 * P2], h0[:, 2 * P2:3 * P2], h0[:, 3 * P2:4 * P2])
    for c in range(S5_LAT_CHUNKS):
        carry = step(lat0 + c * DEC_BATCH, lat0 + (S5_LAT_CHUNKS - 1 - c) * DEC_BATCH, DEC_BATCH, carry)

    y = jnp.dot(u, tz_ref[0], preferred_element_type=f32)
    y = y + jnp.dot(hin_ref[...].astype(bf16), wc_ref[0], preferred_element_type=f32)
    y_ref[0] = y.astype(bf16)


def _s5_scan(u_g, toeplitz, w_state, w_carry, coef, h0):
    G, TH, P2 = S5_GROUPS, S5_T * S5_GROUP, 2 * S5_STATE
    return pl.pallas_call(
        _s5_kernel,
        out_shape=(jax.ShapeDtypeStruct((G, S5_ROWS, TH), bf16),
                   jax.ShapeDtypeStruct((G, BATCH, 2 * P2), f32)),
        grid=(G,),
        in_specs=[
            pl.BlockSpec((1, S5_ROWS, TH), lambda g: (g, 0, 0)),
            pl.BlockSpec((1, TH, TH), lambda g: (g, 0, 0)),
            pl.BlockSpec((1, TH, 4 * P2), lambda g: (g, 0, 0)),
            pl.BlockSpec((1, 2 * P2, TH), lambda g: (g, 0, 0)),
            pl.BlockSpec((1, 8, P2), lambda g: (g, 0, 0)),
            pl.BlockSpec((1, DEC_BATCH, 4 * P2), lambda g: (g, 0, 0)),
        ],
        out_specs=(pl.BlockSpec((1, S5_ROWS, TH), lambda g: (g, 0, 0)),
                   pl.BlockSpec((1, BATCH, 2 * P2), lambda g: (g, 0, 0))),
        scratch_shapes=[pltpu.VMEM((S5_ROWS, 4 * P2), f32), pltpu.VMEM((S5_ROWS, 2 * P2), f32)],
        compiler_params=_cparams(("arbitrary",)),
        name="s5_scan",
    )(u_g, toeplitz, w_state, w_carry, coef, h0)


def _s5_to_groups(u):
    G, H, T = S5_GROUPS, S5_GROUP, S5_T
    ctx = u[:N_CTX].reshape(BATCH, S5_CTX_CHUNKS, T, G, H)
    lat = u[N_CTX:].reshape(DEC_BATCH, S5_LAT_CHUNKS, T, G, H)
    ctx = jnp.transpose(ctx, (3, 1, 0, 2, 4)).reshape(G, S5_CTX_CHUNKS * BATCH, T * H)
    lat = jnp.transpose(lat, (3, 1, 0, 2, 4)).reshape(G, S5_LAT_CHUNKS * DEC_BATCH, T * H)
    return jnp.concatenate([ctx, lat], axis=1)


def _s5_from_groups(y):
    G, H, T = S5_GROUPS, S5_GROUP, S5_T
    nc = S5_CTX_CHUNKS * BATCH
    ctx = y[:, :nc].reshape(G, S5_CTX_CHUNKS, BATCH, T, H)
    lat = y[:, nc:].reshape(G, S5_LAT_CHUNKS, DEC_BATCH, T, H)
    ctx = jnp.transpose(ctx, (2, 1, 3, 0, 4)).reshape(N_CTX, G * H)
    lat = jnp.transpose(lat, (2, 1, 3, 0, 4)).reshape(N_LAT, G * H)
    return jnp.concatenate([ctx, lat], axis=0)


def _s5_init_rows(s_re, s_im):
    parts = [s_re[:, 0], s_im[:, 0], s_im[:, 0], s_re[:, 0], s_re[:, 1], s_im[:, 1], s_im[:, 1], s_re[:, 1]]
    return jnp.transpose(jnp.concatenate(parts, axis=-1), (1, 0, 2))


def _gla_seq_of_chunk(cj):
    per_ctx = SEQ // GLA_C
    per_lat = DEC_SEQ // GLA_C
    return jnp.where(cj < GLA_CTX_CHUNKS, cj // per_ctx, BATCH + (cj - GLA_CTX_CHUNKS) // per_lat)


def _gla_kernel(q_ref, k_ref, v_ref, lg_ref, s0_ref, o_ref, st_ref, a_ref, *, reverse):
    j = pl.program_id(0)
    cj = (GLA_CHUNKS - 1 - j) if reverse else j
    per_ctx = SEQ // GLA_C
    per_lat = DEC_SEQ // GLA_C
    pos = jnp.where(cj < GLA_CTX_CHUNKS, cj % per_ctx, (cj - GLA_CTX_CHUNKS) % per_lat)
    n_in_seq = jnp.where(cj < GLA_CTX_CHUNKS, per_ctx, per_lat)
    is_first = (pos == n_in_seq - 1) if reverse else (pos == 0)

    @pl.when(is_first)
    def _():
        st_ref[...] = s0_ref[...]

    C, SUB = GLA_C, GLA_SUB
    row = lax.broadcasted_iota(i32, (C, C), 0)
    col = lax.broadcasted_iota(i32, (C, C), 1)
    if reverse:
        m_cum = (col >= row)
        m_ref = (col >= (row // SUB + 1) * SUB)
    else:
        m_cum = (col <= row)
        m_ref = (col < (row // SUB) * SUB)
    m_both = jnp.concatenate([m_cum.astype(f32), m_ref.astype(f32)], axis=0)
    lg_all = lg_ref[...]
    sums = jnp.dot(m_both, lg_all, precision=HIGHEST, preferred_element_type=f32)
    scale = GLA_DK ** -0.5
    a_ref[...] = jnp.zeros((C, C), f32)

    for h in range(GLA_HEADS):
        ks = slice(h * GLA_DK, (h + 1) * GLA_DK)
        vs = slice(h * GLA_DV, (h + 1) * GLA_DV)
        lg = lg_all[:, ks]
        cum = sums[0:C, ks]
        ref = sums[C:2 * C, ks]
        tot = jnp.sum(lg, axis=0, keepdims=True)
        q = q_ref[:, ks].astype(f32) * scale
        k = k_ref[:, ks].astype(f32)
        v = v_ref[:, vs]
        state = st_ref[0, h]

        q_in = (q * jnp.exp(cum)).astype(bf16)
        o = jnp.dot(q_in, state.astype(bf16), preferred_element_type=f32)
        k_st = k * jnp.exp(tot - cum)
        tot_col = jnp.sum(lg.T, axis=1, keepdims=True)
        st_ref[0, h] = jnp.exp(tot_col) * state + jnp.dot(k_st.T.astype(bf16), v, preferred_element_type=f32)

        qt = (q * jnp.exp(cum - ref)).astype(bf16)
        for blk in range(GLA_NSUB):
            rows = slice(blk * SUB, (blk + 1) * SUB)
            cols = slice((blk + 1) * SUB, C) if reverse else slice(0, blk * SUB)
            if cols.start == cols.stop:
                continue
            kt = (k[cols] * jnp.exp(ref[blk * SUB:blk * SUB + 1] - cum[cols])).astype(bf16)
            a_ref[rows, cols] = lax.dot_general(qt[rows], kt, (((1,), (1,)), ((), ())), preferred_element_type=f32)

        diag = jnp.zeros((C, C), f32)
        for lag in range(SUB):
            shift = (C - lag) % C if reverse else lag
            k_sh = k if lag == 0 else pltpu.roll(k, shift, axis=0)
            c_sh = cum if lag == 0 else pltpu.roll(cum, shift, axis=0)
            z = q * k_sh * jnp.exp(jnp.minimum(cum - c_sh, 0.0))
            r = jnp.sum(z, axis=1, keepdims=True)
            if reverse:
                hit = (col == row + lag) & (row % SUB + lag < SUB)
            else:
                hit = (col == row - lag) & (row % SUB >= lag)
            diag = diag + jnp.where(hit, r, 0.0)
        scores = (a_ref[...] + diag).astype(bf16)
        o_ref[:, vs] = o + jnp.dot(scores, v, preferred_element_type=f32)


def _gla_direction(proj, logg, s0, reverse):
    dirn = 1 if reverse else 0
    cidx = (lambda j: GLA_CHUNKS - 1 - j) if reverse else (lambda j: j)
    q_blk = (S5_WIDTH) // GLA_QK
    v_blk = (S5_WIDTH + 2 * GLA_QK) // GLA_V
    return pl.pallas_call(
        functools.partial(_gla_kernel, reverse=reverse),
        out_shape=(jax.ShapeDtypeStruct((NT, GLA_V), f32),
                   jax.ShapeDtypeStruct((N_SEQS, GLA_HEADS, GLA_DK, GLA_DV), f32)),
        grid=(GLA_CHUNKS,),
        in_specs=[
            pl.BlockSpec((GLA_C, GLA_QK), lambda j: (cidx(j), q_blk)),
            pl.BlockSpec((GLA_C, GLA_QK), lambda j: (cidx(j), q_blk + 1)),
            pl.BlockSpec((GLA_C, GLA_V), lambda j: (cidx(j), v_blk)),
            pl.BlockSpec((GLA_C, GLA_QK), lambda j: (cidx(j), dirn)),
            pl.BlockSpec((1, GLA_HEADS, GLA_DK, GLA_DV), lambda j: (_gla_seq_of_chunk(cidx(j)), 0, 0, 0)),
        ],
        out_specs=(pl.BlockSpec((GLA_C, GLA_V), lambda j: (cidx(j), 0)),
                   pl.BlockSpec((1, GLA_HEADS, GLA_DK, GLA_DV), lambda j: (_gla_seq_of_chunk(cidx(j)), 0, 0, 0))),
        scratch_shapes=[pltpu.VMEM((GLA_C, GLA_C), f32)],
        compiler_params=_cparams(("arbitrary",)),
        name="gla_bwd" if reverse else "gla_fwd",
    )(proj, proj, proj, logg, s0)


def _post_kernel(y_ref, of_ref, ob_ref, r_ref, ga_ref, gb_ref, x_ref, mod_ref,
                 wglu_ref, bglu_ref, ws5_ref, gn_ref, wgla_ref, wout_ref, n2_ref, wr_ref, br_ref,
                 x1_ref, h2_ref, gate_ref, idx_ref):
    g = jax.nn.gelu(y_ref[...].astype(f32))
    glu = jnp.dot(g.astype(bf16), wglu_ref[...], preferred_element_type=f32) + bglu_ref[...]
    ya = g * jax.nn.sigmoid(glu)

    o = of_ref[...] + ob_ref[...]
    heads = []
    for h in range(GLA_HEADS):
        oh = o[:, h * GLA_DV:(h + 1) * GLA_DV]
        heads.append(oh * lax.rsqrt(jnp.mean(oh * oh, axis=-1, keepdims=True) + EPS))
    r = r_ref[...].astype(f32)
    yb = jnp.concatenate(heads, axis=1) * gn_ref[...] * (r * jax.nn.sigmoid(r))

    merged = (jax.nn.sigmoid(ga_ref[...].astype(f32)) * jnp.dot(ya.astype(bf16), ws5_ref[...], preferred_element_type=f32)
              + jax.nn.sigmoid(gb_ref[...].astype(f32)) * jnp.dot(yb.astype(bf16), wgla_ref[...], preferred_element_type=f32))
    x1 = x_ref[...] + mod_ref[0, 2:3, :] * jnp.dot(merged.astype(bf16), wout_ref[...], preferred_element_type=f32)
    x1_ref[...] = x1

    y2 = x1 * lax.rsqrt(jnp.mean(x1 * x1, axis=-1, keepdims=True) + EPS) * n2_ref[...]
    h2 = y2 * (1.0 + mod_ref[0, 4:5, :]) + mod_ref[0, 3:4, :]
    h2_ref[...] = h2

    logits = jnp.dot(h2, wr_ref[...], precision=HIGHEST, preferred_element_type=f32) + br_ref[...]
    lane = lax.broadcasted_iota(i32, logits.shape, 1)
    lane_f = lane.astype(f32)
    vals = logits
    top_v, top_i = [], []
    for _ in range(TOP_K):
        m = jnp.max(vals, axis=-1, keepdims=True)
        idx = jnp.min(jnp.where(vals == m, lane_f, float(LANES)), axis=-1, keepdims=True).astype(i32)
        top_v.append(m)
        top_i.append(idx)
        vals = jnp.where(lane == idx, -jnp.inf, vals)
    ex = [jnp.exp(v - top_v[0]) for v in top_v]
    inv = 1.0 / (ex[0] + ex[1] + ex[2] + ex[3])
    gates = jnp.zeros(logits.shape, f32)
    idxs = jnp.zeros(logits.shape, i32)
    for kk in range(TOP_K):
        gates = jnp.where(lane == kk, ex[kk] * inv, gates)
        idxs = jnp.where(lane == kk, top_i[kk], idxs)
    gate_ref[...] = gates
    idx_ref[...] = idxs


def _post(y_s5, o_f, o_b, proj, x, mod_l, wglu, bglu, ws5, gnorm, wgla, wout, norm2, wr, br):
    const = lambda shape: pl.BlockSpec(shape, lambda i: (0,) * len(shape), pipeline_mode=pl.Buffered(1))
    r_blk = (S5_WIDTH + 2 * GLA_QK + GLA_V) // GLA_V
    ga_blk = 4096 // D_MODEL
    return pl.pallas_call(
        _post_kernel,
        out_shape=(jax.ShapeDtypeStruct((NT, D_MODEL), f32),
                   jax.ShapeDtypeStruct((NT, D_MODEL), f32),
                   jax.ShapeDtypeStruct((NT, LANES), f32),
                   jax.ShapeDtypeStruct((NT, LANES), i32)),
        grid=(NT // POST_TM,),
        in_specs=[
            pl.BlockSpec((POST_TM, S5_WIDTH), lambda i: (i, 0)),
            pl.BlockSpec((POST_TM, GLA_V), lambda i: (i, 0)),
            pl.BlockSpec((POST_TM, GLA_V), lambda i: (i, 0)),
            pl.BlockSpec((POST_TM, GLA_V), lambda i: (i, r_blk)),
            pl.BlockSpec((POST_TM, D_MODEL), lambda i: (i, ga_blk)),
            pl.BlockSpec((POST_TM, D_MODEL), lambda i: (i, ga_blk + 1)),
            pl.BlockSpec((POST_TM, D_MODEL), lambda i: (i, 0)),
            pl.BlockSpec((1, 6, D_MODEL), lambda i: (_row_seq_of_tile(i, POST_TM), 0, 0)),
            const((S5_WIDTH, S5_WIDTH)), const((1, S5_WIDTH)), const((S5_WIDTH, D_MODEL)),
            const((1, GLA_V)), const((GLA_V, D_MODEL)), const((D_MODEL, D_MODEL)), const((1, D_MODEL)),
            const((D_MODEL, LANES)), const((1, LANES)),
        ],
        out_specs=(pl.BlockSpec((POST_TM, D_MODEL), lambda i: (i, 0)),
                   pl.BlockSpec((POST_TM, D_MODEL), lambda i: (i, 0)),
                   pl.BlockSpec((POST_TM, LANES), lambda i: (i, 0)),
                   pl.BlockSpec((POST_TM, LANES), lambda i: (i, 0))),
        compiler_params=_cparams(("arbitrary",)),
        name="post",
    )(y_s5, o_f, o_b, proj, proj, proj, x, mod_l, wglu, bglu, ws5, gnorm, wgla, wout, norm2, wr, br)


def _routing(top_idx):
    m = NT * TOP_K
    e_flat = top_idx.reshape(-1)
    experts = jnp.arange(N_EXPERTS, dtype=i32)
    onehot = (e_flat[:, None] == experts[None, :]).astype(i32)
    csum = jnp.cumsum(onehot, axis=0)
    counts = csum[-1]
    starts = jnp.cumsum(counts) - counts
    nblk = (counts + MOE_BM - 1) // MOE_BM
    blk_end = jnp.cumsum(nblk)
    blk_start = blk_end - nblk
    dest = jnp.sum(onehot * (blk_start[None, :] * MOE_BM + csum - 1), axis=1).astype(i32)

    order = jnp.argsort(e_flat, stable=True).astype(i32)
    p = jnp.arange(MOE_ROWS, dtype=i32)
    e_p = jnp.minimum(jnp.sum((blk_end[None, :] <= (p // MOE_BM)[:, None]).astype(i32), axis=1), N_EXPERTS - 1)
    oh_p = (e_p[:, None] == experts[None, :]).astype(i32)
    pick = lambda v: jnp.sum(oh_p * v[None, :], axis=1)
    idx_in = p - pick(blk_start) * MOE_BM
    valid = (p < blk_end[-1] * MOE_BM) & (idx_in < pick(counts))
    src = order[jnp.clip(pick(starts) + idx_in, 0, m - 1)] // TOP_K
    row_tok = jnp.where(valid, src, 0).astype(i32)

    n_units = (nblk + MOE_RB - 1) // MOE_RB
    unit_end = jnp.cumsum(n_units)
    unit_start = unit_end - n_units
    total_units = unit_end[-1]
    u = jnp.arange(MOE_UNITS, dtype=i32)
    used = u < total_units
    e_of_u = jnp.minimum(jnp.sum((unit_end[None, :] <= u[:, None]).astype(i32), axis=1), N_EXPERTS - 1)
    local = u - unit_start[e_of_u]
    u_blk0 = blk_start[e_of_u] + local * MOE_RB
    u_nblk = jnp.minimum(MOE_RB, nblk[e_of_u] - local * MOE_RB)
    last_e = e_of_u[jnp.maximum(total_units - 1, 0)]
    u_exp = jnp.where(used, e_of_u, last_e).astype(i32)
    u_blk0 = jnp.where(used, u_blk0, 0).astype(i32)
    u_nblk = jnp.where(used, u_nblk, 0).astype(i32)
    return row_tok, dest, u_exp, u_blk0, u_nblk, blk_end[-1:].astype(i32)


def _gather_kernel(tok_ref, src_ref, o_ref, stage_ref, sem):
    base = pl.program_id(0) * GATHER_GB
    S = GATHER_SPLIT

    def row_copy(t, r):
        return pltpu.make_async_copy(src_ref.at[pl.ds(pl.multiple_of(t * S, S), S), :],
                                     stage_ref.at[pl.ds(pl.multiple_of(r * S, S), S), :], sem)

    def issue(r, c):
        row_copy(tok_ref[base + r], r).start()
        return c

    lax.fori_loop(0, GATHER_GB, issue, 0)

    def wait(r, c):
        row_copy(0, r).wait()
        return c

    lax.fori_loop(0, GATHER_GB, wait, 0)
    parts = [stage_ref[pl.ds(k, GATHER_GB, stride=S), :] for k in range(S)]
    o_ref[...] = jnp.concatenate(parts, axis=1).astype(bf16)


def _gather_rows(h2, row_tok):
    S = GATHER_SPLIT
    return pl.pallas_call(
        _gather_kernel,
        out_shape=jax.ShapeDtypeStruct((MOE_ROWS, D_MODEL), bf16),
        grid_spec=pltpu.PrefetchScalarGridSpec(
            num_scalar_prefetch=1,
            grid=(MOE_ROWS // GATHER_GB,),
            in_specs=[pl.BlockSpec(memory_space=pl.ANY)],
            out_specs=pl.BlockSpec((GATHER_GB, D_MODEL), lambda i, tok: (i, 0)),
            scratch_shapes=[pltpu.VMEM((GATHER_GB * S, D_MODEL // S), f32), pltpu.SemaphoreType.DMA(())],
        ),
        compiler_params=_cparams(("arbitrary",)),
        name="moe_gather",
    )(row_tok, h2.reshape(NT * S, D_MODEL // S))


def _deinterleave_matrix():
    half = MOE_TA // 2
    src = jnp.arange(MOE_TA)[:, None]
    dst = jnp.arange(MOE_TA)[None, :]
    return (dst == (src % 2) * half + src // 2).astype(bf16)


def _moe_kernel(uexp_ref, ublk_ref, unb_ref, nused_ref, x_hbm, w1_ref, b1_ref, perm_ref, w2_ref, b2_ref,
                y_hbm, x_buf, act_buf, w_buf, y_buf, zero_buf, sem_in, sem_out):
    u = pl.program_id(0)
    j = pl.program_id(1)
    nb = unb_ref[u]
    row0 = ublk_ref[u] * MOE_BM
    big = MOE_QB * MOE_BM
    n_big = nb // MOE_QB
    n_small = nb - n_big * MOE_QB
    small0 = n_big * big

    def in_copy(r):
        rows = pl.ds(pl.multiple_of(r * MOE_BM, MOE_BM), MOE_BM)
        src = x_hbm.at[pl.ds(pl.multiple_of(row0 + r * MOE_BM, MOE_BM), MOE_BM), :]
        return pltpu.make_async_copy(src, x_buf.at[rows, :], sem_in)

    @pl.when((j == 0) & (nb > 0))
    def _():
        lax.fori_loop(0, nb, lambda r, c: (in_copy(r).start(), c)[1], 0)
        lax.fori_loop(0, nb, lambda r, c: (in_copy(r).wait(), c)[1], 0)

    @pl.when((j < MOE_J1) & (nb > 0))
    def _():
        w_buf[...] = w1_ref[0, 0].astype(bf16)
        half = MOE_TA // 2

        def act_rows(start, m):
            rows = pl.ds(pl.multiple_of(start, MOE_BM), m)
            h = jnp.dot(x_buf[rows, :], w_buf[...], preferred_element_type=f32) + b1_ref[0, 0]
            hb = h.astype(bf16)
            parts = []
            for s in range(MOE_TN // MOE_TA):
                hp = jnp.dot(hb[:, s * MOE_TA:(s + 1) * MOE_TA], perm_ref[...], preferred_element_type=f32)
                hg = jnp.minimum(hp[:, :half], SWIGLU_LIMIT)
                hl = jnp.clip(hp[:, half:], -SWIGLU_LIMIT, SWIGLU_LIMIT)
                parts.append(hg * jax.nn.sigmoid(SWIGLU_ALPHA * hg) * (hl + 1.0))
            act_buf[j, rows, :] = jnp.concatenate(parts, axis=1).astype(bf16)

        lax.fori_loop(0, n_big, lambda q, c: (act_rows(q * big, big), c)[1], 0)
        lax.fori_loop(0, n_small, lambda r, c: (act_rows(small0 + r * MOE_BM, MOE_BM), c)[1], 0)

    @pl.when((j >= MOE_J1) & (nb > 0))
    def _():
        w_buf[...] = w2_ref[0, 0].astype(bf16)
        col0 = pl.multiple_of((j - MOE_J1) * MOE_TN, MOE_TN)

        def out_copy(start, m):
            rows = pl.ds(pl.multiple_of(start, MOE_BM), m)
            dst = y_hbm.at[pl.ds(pl.multiple_of(row0 + start, MOE_BM), m), pl.ds(col0, MOE_TN)]
            return pltpu.make_async_copy(y_buf.at[rows, :], dst, sem_out)

        def out_rows(start, m):
            rows = pl.ds(pl.multiple_of(start, MOE_BM), m)
            acc = b2_ref[0, 0] + jnp.zeros((m, MOE_TN), f32)
            for jj in range(MOE_J1):
                acc = acc + jnp.dot(act_buf[jj, rows, :], w_buf[jj * MOE_TA:(jj + 1) * MOE_TA, :],
                                    preferred_element_type=f32)
            y_buf[rows, :] = acc
            out_copy(start, m).start()

        lax.fori_loop(0, n_big, lambda q, c: (out_rows(q * big, big), c)[1], 0)
        lax.fori_loop(0, n_small, lambda r, c: (out_rows(small0 + r * MOE_BM, MOE_BM), c)[1], 0)
        lax.fori_loop(0, n_big, lambda q, c: (out_copy(q * big, big).wait(), c)[1], 0)
        lax.fori_loop(0, n_small, lambda r, c: (out_copy(small0 + r * MOE_BM, MOE_BM).wait(), c)[1], 0)

    @pl.when((u == MOE_UNITS - 1) & (j == MOE_J1 + MOE_J2 - 1))
    def _():
        zero_buf[...] = jnp.zeros(zero_buf.shape, f32)

        def fill_copy(r):
            dst = y_hbm.at[pl.ds(pl.multiple_of(r * MOE_BM, MOE_BM), MOE_BM), :]
            return pltpu.make_async_copy(zero_buf, dst, sem_out)

        lax.fori_loop(nused_ref[0], MOE_BLOCKS, lambda r, c: (fill_copy(r).start(), c)[1], 0)
        lax.fori_loop(nused_ref[0], MOE_BLOCKS, lambda r, c: (fill_copy(r).wait(), c)[1], 0)


def _moe_experts(x_sorted, u_exp, u_blk0, u_nblk, n_used, w1, b1, w2, b2, layer):
    def w1_idx(u, j, uexp, ublk, unb, nused):
        return (layer, uexp[u], 0, jnp.where(unb[u] > 0, jnp.minimum(j, MOE_J1 - 1), MOE_J1 - 1))

    def w2_idx(u, j, uexp, ublk, unb, nused):
        return (layer, uexp[u], 0, jnp.where(unb[u] > 0, jnp.maximum(j - MOE_J1, 0), MOE_J2 - 1))

    return pl.pallas_call(
        _moe_kernel,
        out_shape=jax.ShapeDtypeStruct((MOE_ROWS, D_MODEL), f32),
        grid_spec=pltpu.PrefetchScalarGridSpec(
            num_scalar_prefetch=4,
            grid=(MOE_UNITS, MOE_J1 + MOE_J2),
            in_specs=[
                pl.BlockSpec(memory_space=pl.ANY),
                pl.BlockSpec((1, 1, D_MODEL, MOE_TN), w1_idx),
                pl.BlockSpec((1, 1, 1, MOE_TN), w1_idx),
                pl.BlockSpec((MOE_TA, MOE_TA), lambda u, j, *_: (0, 0)),
                pl.BlockSpec((1, 1, D_MODEL, MOE_TN), w2_idx),
                pl.BlockSpec((1, 1, 1, MOE_TN), w2_idx),
            ],
            out_specs=pl.BlockSpec(memory_space=pl.ANY),
            scratch_shapes=[
                pltpu.VMEM((MOE_RMAX, D_MODEL), bf16),
                pltpu.VMEM((MOE_J1, MOE_RMAX, MOE_TA), bf16),
                pltpu.VMEM((D_MODEL, MOE_TN), bf16),
                pltpu.VMEM((MOE_RMAX, MOE_TN), f32),
                pltpu.VMEM((MOE_BM, D_MODEL), f32),
                pltpu.SemaphoreType.DMA(()),
                pltpu.SemaphoreType.DMA(()),
            ],
        ),
        compiler_params=_cparams(("arbitrary", "arbitrary")),
        name="moe_experts",
    )(u_exp, u_blk0, u_nblk, n_used, x_sorted, w1, b1, _deinterleave_matrix(), w2, b2)


def _combine_kernel(dest_ref, y_hbm, gate_ref, x1_ref, mod_ref, nf_ref, o_ref, y_buf, sem, *, final_norm):
    base = pl.program_id(0) * COMB_TM * TOP_K

    def row_copy(p, kk, r):
        return pltpu.make_async_copy(y_hbm.at[pl.ds(p, 1), :], y_buf.at[kk, pl.ds(r, 1), :], sem)

    def issue(r, c):
        for kk in range(TOP_K):
            row_copy(dest_ref[base + r * TOP_K + kk], kk, r).start()
        return c

    lax.fori_loop(0, COMB_TM, issue, 0)

    def wait(r, c):
        for kk in range(TOP_K):
            row_copy(0, kk, r).wait()
        return c

    lax.fori_loop(0, COMB_TM, wait, 0)

    gates = gate_ref[...]
    acc = jnp.zeros((COMB_TM, D_MODEL), f32)
    for kk in range(TOP_K):
        acc = acc + gates[:, kk:kk + 1] * y_buf[kk]
    x2 = x1_ref[...] + mod_ref[0, 5:6, :] * acc
    if final_norm:
        x2 = x2 * lax.rsqrt(jnp.mean(x2 * x2, axis=-1, keepdims=True) + EPS) * nf_ref[...]
    o_ref[...] = x2


def _combine(y_sorted, dest, gates, x1, mod_l, norm_f, final_norm):
    return pl.pallas_call(
        functools.partial(_combine_kernel, final_norm=final_norm),
        out_shape=jax.ShapeDtypeStruct((NT, D_MODEL), f32),
        grid_spec=pltpu.PrefetchScalarGridSpec(
            num_scalar_prefetch=1,
            grid=(NT // COMB_TM,),
            in_specs=[
                pl.BlockSpec(memory_space=pl.ANY),
                pl.BlockSpec((COMB_TM, LANES), lambda i, d: (i, 0)),
                pl.BlockSpec((COMB_TM, D_MODEL), lambda i, d: (i, 0)),
                pl.BlockSpec((1, 6, D_MODEL), lambda i, d: (_row_seq_of_tile(i, COMB_TM), 0, 0)),
                pl.BlockSpec((1, D_MODEL), lambda i, d: (0, 0)),
            ],
            out_specs=pl.BlockSpec((COMB_TM, D_MODEL), lambda i, d: (i, 0)),
            scratch_shapes=[pltpu.VMEM((TOP_K, COMB_TM, D_MODEL), f32), pltpu.SemaphoreType.DMA(())],
        ),
        compiler_params=_cparams(("arbitrary",)),
        name="moe_combine_final" if final_norm else "moe_combine",
    )(dest, y_sorted, gates, x1, mod_l, norm_f)


def kernel(x_prompt, x_sample, c, state_s5_re, state_s5_im, state_gla, c_ctx, w_mod, b_mod, norm1, w_in, s5_lam_re, s5_lam_im, s5_log_dt, s5_b_re, s5_b_im, s5_c_re, s5_c_im, s5_d, s5_w_glu, s5_b_glu, w_s5_out, gla_w_g2, gla_b_g, gla_norm, w_gla_out, w_out, norm2, w_router, b_router, w_e1, b_e1, w_e2, b_e2, norm_f):
    cond = jnp.concatenate([c_ctx[None], c, jnp.zeros((N_MODROWS - 1 - DEC_BATCH, D_MODEL), f32)], axis=0)
    mod = _modulation(cond, w_mod, b_mod).reshape(DEPTH, N_MODROWS, 6, D_MODEL)
    x = _embed(x_prompt, x_sample, _grid_pos_embed())

    glr0 = S5_WIDTH + 2 * GLA_QK + 2 * GLA_V
    glr1 = glr0 + N_DIRS * GLA_RANK
    s5_re_out, s5_im_out, gla_out = [], [], []
    b_e1_r = b_e1.reshape(DEPTH, N_EXPERTS, 1, 2 * D_MODEL)
    b_e2_r = b_e2.reshape(DEPTH, N_EXPERTS, 1, D_MODEL)
    for l in range(DEPTH):
        mod_l = mod[l]
        w_main = jnp.concatenate([w_in[l, :, :glr0], w_in[l, :, glr1:]], axis=1).astype(bf16)
        w_glr = w_in[l, :, glr0:glr1].astype(bf16)
        zg = jnp.zeros((GLA_RANK, GLA_QK), f32)
        w_g2bd = jnp.concatenate([jnp.concatenate([gla_w_g2[l, 0], zg], axis=1),
                                  jnp.concatenate([zg, gla_w_g2[l, 1]], axis=1)], axis=0)
        b_g = gla_b_g[l].reshape(1, N_DIRS * GLA_QK)
        toeplitz, w_state, w_carry, coef = _s5_weights(
            s5_lam_re[l], s5_lam_im[l], s5_log_dt[l], s5_b_re[l], s5_b_im[l], s5_c_re[l], s5_c_im[l], s5_d[l])
        h0 = _s5_init_rows(state_s5_re[:, l], state_s5_im[:, l])
        zero_state = jnp.zeros((BATCH, GLA_HEADS, GLA_DK, GLA_DV), f32)
        w_r = jnp.concatenate([w_router[l], jnp.zeros((D_MODEL, LANES - N_EXPERTS), f32)], axis=1)
        b_r = jnp.concatenate([b_router[l], jnp.full((LANES - N_EXPERTS,), -jnp.inf, f32)]).reshape(1, LANES)

        proj, logg = _pre(x, mod_l, norm1[l].reshape(1, D_MODEL), w_main, w_glr, w_g2bd, b_g)
        y_g, s5_fin = _s5_scan(_s5_to_groups(proj[:, :S5_WIDTH]), toeplitz, w_state, w_carry, coef, h0)
        y_s5 = _s5_from_groups(y_g)
        o_f, st_f = _gla_direction(proj, logg, jnp.concatenate([zero_state, state_gla[:, l, 0]], axis=0), False)
        o_b, st_b = _gla_direction(proj, logg, jnp.concatenate([zero_state, state_gla[:, l, 1]], axis=0), True)

        x1, h2, gates, top_idx = _post(
            y_s5, o_f, o_b, proj, x, mod_l,
            s5_w_glu[l].astype(bf16), s5_b_glu[l].reshape(1, S5_WIDTH), w_s5_out[l].astype(bf16),
            jnp.tile(gla_norm[l], GLA_HEADS).reshape(1, GLA_V), w_gla_out[l].astype(bf16), w_out[l].astype(bf16),
            norm2[l].reshape(1, D_MODEL), w_r, b_r)

        row_tok, dest, u_exp, u_blk0, u_nblk, n_used = _routing(top_idx[:, :TOP_K])
        x_sorted = _gather_rows(h2, row_tok)
        y_sorted = _moe_experts(x_sorted, u_exp, u_blk0, u_nblk, n_used, w_e1, b_e1_r, w_e2, b_e2_r, l)
        x = _combine(y_sorted, dest, gates, x1, mod_l, norm_f.reshape(1, D_MODEL), l == DEPTH - 1)

        P = S5_STATE
        fin = jnp.transpose(s5_fin, (1, 0, 2))
        s5_re_out.append(jnp.stack([fin[..., 0:P], fin[..., 2 * P:3 * P]], axis=1))
        s5_im_out.append(jnp.stack([fin[..., P:2 * P], fin[..., 3 * P:4 * P]], axis=1))
        gla_out.append(jnp.stack([st_f[:BATCH], st_b[:BATCH]], axis=1))

    y_prompt = x[:N_CTX].reshape(BATCH, SEQ, D_MODEL)
    y_sample = x[N_CTX:].reshape(DEC_BATCH, DEC_SEQ, D_MODEL)
    return (y_prompt, y_sample, jnp.stack(s5_re_out, axis=1), jnp.stack(s5_im_out, axis=1),
            jnp.stack(gla_out, axis=1))
```

```python
import functools
import math

import jax
import jax.numpy as jnp
from jax import lax
from jax.experimental import pallas as pl
from jax.experimental.pallas import tpu as pltpu

f32 = jnp.float32
bf16 = jnp.bfloat16
i32 = jnp.int32
HIGHEST = lax.Precision.HIGHEST

D_MODEL = 2048
BATCH = 16
SEQ = 256
DEPTH = 2
DEC_BATCH = 2
DEC_SEQ = 2048
GRID_W = 64
N_DIRS = 2
S5_WIDTH = 1024
S5_GROUP = 16
S5_GROUPS = 64
S5_STATE = 64
GLA_HEADS = 4
GLA_DK = 128
GLA_DV = 256
GLA_QK = 512
GLA_V = 1024
GLA_RANK = 16
GLA_GATE_NORM = 16.0
N_EXPERTS = 32
TOP_K = 4
SWIGLU_LIMIT = 7.0
SWIGLU_ALPHA = 1.702
POS_BASE = 10000.0
EPS = 1e-6

N_CTX = BATCH * SEQ
N_LAT = DEC_BATCH * DEC_SEQ
NT = N_CTX + N_LAT
N_SEQS = BATCH + DEC_BATCH
N_MODROWS = 8

LANES = 128
VMEM_LIMIT = 56 * 1024 * 1024

S5_T = 16
S5_ROWS = NT // S5_T
S5_CTX_CHUNKS = SEQ // S5_T
S5_LAT_CHUNKS = DEC_SEQ // S5_T
GLA_C = 128
GLA_SUB = 16
GLA_NSUB = GLA_C // GLA_SUB
GLA_CHUNKS = NT // GLA_C
GLA_CTX_CHUNKS = N_CTX // GLA_C
PRE_TM = 1024
PRE_TN = 1024
PROJ_COLS = 8192
POST_TM = 256
MOE_BM = 128
MOE_ROWS = NT * TOP_K + N_EXPERTS * MOE_BM
MOE_BLOCKS = MOE_ROWS // MOE_BM
MOE_RB = 16
MOE_QB = 4
MOE_RMAX = MOE_RB * MOE_BM
MOE_UNITS = N_EXPERTS + MOE_BLOCKS // MOE_RB
MOE_TN = 512
MOE_TA = MOE_TN // 2
MOE_J1 = 2 * D_MODEL // MOE_TN
MOE_J2 = D_MODEL // MOE_TN
GATHER_GB = 512
GATHER_SPLIT = D_MODEL // LANES
COMB_TM = 128


def _cparams(sem, **kw):
    return pltpu.CompilerParams(dimension_semantics=sem, vmem_limit_bytes=VMEM_LIMIT, **kw)


def _row_seq_of_tile(i, tile_rows):
    first = i * tile_rows
    return jnp.where(first < N_CTX, 0, 1 + (first - N_CTX) // DEC_SEQ)


def _log_sigmoid(x):
    return -(jnp.maximum(-x, 0.0) + jnp.log1p(jnp.exp(-jnp.abs(x))))


MOD_TN = 1024


def _mod_kernel(c_ref, w_ref, b_ref, o_ref):
    c = c_ref[...]
    s = (c * jax.nn.sigmoid(c)).astype(bf16)
    o_ref[0] = jnp.dot(s, w_ref[0].astype(bf16), preferred_element_type=f32) + b_ref[0]


def _modulation(cond, w_mod, b_mod):
    n_out = 6 * D_MODEL
    return pl.pallas_call(
        _mod_kernel,
        out_shape=jax.ShapeDtypeStruct((DEPTH, N_MODROWS, n_out), f32),
        grid=(DEPTH, n_out // MOD_TN),
        in_specs=[
            pl.BlockSpec((N_MODROWS, D_MODEL), lambda l, j: (0, 0)),
            pl.BlockSpec((1, D_MODEL, MOD_TN), lambda l, j: (l, 0, j)),
            pl.BlockSpec((1, 1, MOD_TN), lambda l, j: (l, 0, j)),
        ],
        out_specs=pl.BlockSpec((1, N_MODROWS, MOD_TN), lambda l, j: (l, 0, j)),
        compiler_params=_cparams(("arbitrary", "arbitrary")),
        name="modulation",
    )(cond, w_mod, b_mod.reshape(DEPTH, 1, n_out))


EMB_TM = 512


def _embed_kernel(xp_ref, xs_ref, pe_ref, o_ref):
    i = pl.program_id(0)

    @pl.when(i < N_CTX // EMB_TM)
    def _():
        o_ref[...] = xp_ref[...]

    @pl.when(i >= N_CTX // EMB_TM)
    def _():
        o_ref[...] = xs_ref[...] + pe_ref[...]


def _embed(x_prompt, x_sample, pe):
    nc = N_CTX // EMB_TM
    per_seq = DEC_SEQ // EMB_TM
    return pl.pallas_call(
        _embed_kernel,
        out_shape=jax.ShapeDtypeStruct((NT, D_MODEL), f32),
        grid=(NT // EMB_TM,),
        in_specs=[
            pl.BlockSpec((EMB_TM, D_MODEL), lambda i: (jnp.minimum(i, nc - 1), 0)),
            pl.BlockSpec((EMB_TM, D_MODEL), lambda i: (jnp.maximum(i - nc, 0), 0)),
            pl.BlockSpec((EMB_TM, D_MODEL), lambda i: (jnp.maximum(i - nc, 0) % per_seq, 0)),
        ],
        out_specs=pl.BlockSpec((EMB_TM, D_MODEL), lambda i: (i, 0)),
        compiler_params=_cparams(("arbitrary",)),
        name="embed",
    )(x_prompt.reshape(N_CTX, D_MODEL), x_sample.reshape(N_LAT, D_MODEL), pe)


def _grid_pos_embed():
    rows = DEC_SEQ // GRID_W
    rr, cc = jnp.meshgrid(jnp.arange(rows, dtype=f32), jnp.arange(GRID_W, dtype=f32), indexing="ij")
    quarter = D_MODEL // 4
    omega = 1.0 / (POS_BASE ** (jnp.arange(quarter, dtype=f32) / quarter))

    def emb(pos):
        ang = pos[:, None] * omega[None, :]
        return jnp.concatenate([jnp.sin(ang), jnp.cos(ang)], axis=-1)

    return jnp.concatenate([emb(rr.reshape(-1)), emb(cc.reshape(-1))], axis=-1)


def _pre_kernel(x_ref, mod_ref, g_ref, w_ref, wglr_ref, wg2_ref, bg_ref, proj_ref, logg_ref, h_ref):
    @pl.when(pl.program_id(1) == 0)
    def _():
        x = x_ref[...]
        y = x * lax.rsqrt(jnp.mean(x * x, axis=-1, keepdims=True) + EPS) * g_ref[...]
        h = y * (1.0 + mod_ref[0, 1:2, :]) + mod_ref[0, 0:1, :]
        hb = h.astype(bf16)
        h_ref[...] = hb
        glr = jnp.dot(hb, wglr_ref[...], preferred_element_type=f32)
        gate = jnp.dot(glr, wg2_ref[...], precision=HIGHEST, preferred_element_type=f32) + bg_ref[...]
        logg_ref[...] = _log_sigmoid(gate) * (1.0 / GLA_GATE_NORM)

    proj_ref[...] = jnp.dot(h_ref[...], w_ref[...], preferred_element_type=f32).astype(bf16)


def _pre(x, mod_l, norm1, w_main, w_glr, w_g2bd, b_g):
    return pl.pallas_call(
        _pre_kernel,
        out_shape=(jax.ShapeDtypeStruct((NT, PROJ_COLS), bf16),
                   jax.ShapeDtypeStruct((NT, N_DIRS * GLA_QK), f32)),
        grid=(NT // PRE_TM, PROJ_COLS // PRE_TN),
        in_specs=[
            pl.BlockSpec((PRE_TM, D_MODEL), lambda i, j: (i, 0)),
            pl.BlockSpec((1, 6, D_MODEL), lambda i, j: (_row_seq_of_tile(i, PRE_TM), 0, 0)),
            pl.BlockSpec((1, D_MODEL), lambda i, j: (0, 0)),
            pl.BlockSpec((D_MODEL, PRE_TN), lambda i, j: (0, j)),
            pl.BlockSpec((D_MODEL, N_DIRS * GLA_RANK), lambda i, j: (0, 0)),
            pl.BlockSpec((N_DIRS * GLA_RANK, N_DIRS * GLA_QK), lambda i, j: (0, 0)),
            pl.BlockSpec((1, N_DIRS * GLA_QK), lambda i, j: (0, 0)),
        ],
        out_specs=(pl.BlockSpec((PRE_TM, PRE_TN), lambda i, j: (i, j)),
                   pl.BlockSpec((PRE_TM, N_DIRS * GLA_QK), lambda i, j: (i, 0))),
        scratch_shapes=[pltpu.VMEM((PRE_TM, D_MODEL), bf16)],
        compiler_params=_cparams(("arbitrary", "arbitrary")),
        name="pre",
    )(x, mod_l, norm1, w_main, w_glr, w_g2bd, b_g)


def _s5_weights(lam_re, lam_im, log_dt, b_re, b_im, c_re, c_im, d):
    T, G, P, H = S5_T, S5_GROUPS, S5_STATE, S5_GROUP
    dt = jnp.exp(log_dt)[..., None]
    z_re, z_im = lam_re * dt, lam_im * dt
    mag = jnp.exp(z_re)
    a_re, a_im = mag * jnp.cos(z_im), mag * jnp.sin(z_im)
    den = lam_re * lam_re + lam_im * lam_im
    n_re, n_im = a_re - 1.0, a_im
    k_re = (n_re * lam_re + n_im * lam_im) / den
    k_im = (n_im * lam_re - n_re * lam_im) / den
    bb_re = k_re[..., None] * b_re - k_im[..., None] * b_im
    bb_im = k_re[..., None] * b_im + k_im[..., None] * b_re
    n = jnp.arange(T + 1, dtype=f32)[:, None, None, None]
    pmag = jnp.exp(n * z_re)
    pw_re, pw_im = pmag * jnp.cos(n * z_im), pmag * jnp.sin(n * z_im)
    abb_re = pw_re[..., None] * bb_re - pw_im[..., None] * bb_im
    abb_im = pw_re[..., None] * bb_im + pw_im[..., None] * bb_re
    kern = (jnp.einsum("dgop,ndgpi->ndgoi", c_re, abb_re[:T], precision=HIGHEST)
            - jnp.einsum("dgop,ndgpi->ndgoi", c_im, abb_im[:T], precision=HIGHEST))

    s_idx = jnp.arange(T)[:, None]
    t_idx = jnp.arange(T)[None, :]
    lag_f = t_idx - s_idx
    kf = jnp.where((lag_f >= 0)[:, :, None, None, None], kern[jnp.clip(lag_f, 0, T - 1), 0], 0.0)
    kb = jnp.where((lag_f <= 0)[:, :, None, None, None], kern[jnp.clip(-lag_f, 0, T - 1), 1], 0.0)
    eye_t = jnp.eye(T, dtype=f32)[:, :, None, None, None]
    eye_h = jnp.eye(H, dtype=f32)[None, None, None, :, :]
    dterm = eye_t * eye_h * d[None, None, :, :, None]
    toeplitz = jnp.transpose(kf + kb + dterm, (2, 0, 4, 1, 3)).reshape(G, T * H, T * H)

    def state_cols(powers, dirn):
        re = jnp.transpose(abb_re[powers, dirn], (1, 0, 3, 2)).reshape(G, T * H, P)
        im = jnp.transpose(abb_im[powers, dirn], (1, 0, 3, 2)).reshape(G, T * H, P)
        return [re, im, im, re]

    w_state = jnp.concatenate(state_cols(T - 1 - jnp.arange(T), 0) + state_cols(jnp.arange(T), 1), axis=-1)

    def carry_rows(powers, dirn):
        pr = pw_re[powers, dirn]
        pi = pw_im[powers, dirn]
        cr, ci = c_re[dirn], c_im[dirn]
        on_re = cr[None] * pr[:, :, None, :] - ci[None] * pi[:, :, None, :]
        on_im = -cr[None] * pi[:, :, None, :] - ci[None] * pr[:, :, None, :]
        to_cols = lambda m: jnp.transpose(m, (1, 3, 0, 2)).reshape(G, P, T * H)
        return [to_cols(on_re), to_cols(on_im)]

    w_carry = jnp.concatenate(carry_rows(1 + jnp.arange(T), 0) + carry_rows(T - jnp.arange(T), 1), axis=1)

    def coef_rows(dirn):
        ar, ai = pw_re[T, dirn], pw_im[T, dirn]
        return [jnp.concatenate([ar, ar], -1), jnp.concatenate([-ai, ai], -1), jnp.concatenate([ai, -ai], -1)]

    zero = jnp.zeros((G, 2 * P), f32)
    coef = jnp.stack(coef_rows(0) + coef_rows(1) + [zero, zero], axis=1)
    return toeplitz.astype(bf16), w_state.astype(bf16), w_carry.astype(bf16), coef


def _s5_kernel(u_ref, tz_ref, ws_ref, wc_ref, coef_ref, h0_ref, y_ref, fin_ref, s_ref, hin_ref):
    u = u_ref[0]
    s_ref[...] = jnp.dot(u, ws_ref[0], preferred_element_type=f32)
    co = coef_ref[0]
    a1f, a2f, a2sf = co[0:1], co[1:2], co[2:3]
    a1b, a2b, a2sb = co[3:4], co[4:5], co[5:6]
    P2 = 2 * S5_STATE

    def step(rf, rb, n_rows, carry):
        hf, hfs, hb, hbs = carry
        hin_ref[pl.ds(rf, n_rows), 0:P2] = hf
        hin_ref[pl.ds(rb, n_rows), P2:2 * P2] = hb
        sf = s_ref[pl.ds(rf, n_rows), 0:P2]
        sfs = s_ref[pl.ds(rf, n_rows), P2:2 * P2]
        sb = s_ref[pl.ds(rb, n_rows), 2 * P2:3 * P2]
        sbs = s_ref[pl.ds(rb, n_rows), 3 * P2:4 * P2]
        return (a1f * hf + a2f * hfs + sf, a1f * hfs + a2sf * hf + sfs,
                a1b * hb + a2b * hbs + sb, a1b * hbs + a2sb * hb + sbs)

    zero = jnp.zeros((BATCH, P2), f32)
    carry = (zero, zero, zero, zero)
    for c in range(S5_CTX_CHUNKS):
        carry = step(c * BATCH, (S5_CTX_CHUNKS - 1 - c) * BATCH, BATCH, carry)
    fin_ref[0] = jnp.concatenate([carry[0], carry[2]], axis=1)

    h0 = h0_ref[0]
    lat0 = S5_CTX_CHUNKS * BATCH

    carry = (h0[:, 0:P2], h0[:, P2:2 * P2], h0[:, 2 * P2:3 * P2], h0[:, 3 * P2:4 * P2])
    for c in range(S5_LAT_CHUNKS):
        carry = step(lat0 + c * DEC_BATCH, lat0 + (S5_LAT_CHUNKS - 1 - c) * DEC_BATCH, DEC_BATCH, carry)

    y = jnp.dot(u, tz_ref[0], preferred_element_type=f32)
    y = y + jnp.dot(hin_ref[...].astype(bf16), wc_ref[0], preferred_element_type=f32)
    y_ref[0] = y.astype(bf16)


def _s5_scan(u_g, toeplitz, w_state, w_carry, coef, h0):
    G, TH, P2 = S5_GROUPS, S5_T * S5_GROUP, 2 * S5_STATE
    return pl.pallas_call(
        _s5_kernel,
        out_shape=(jax.ShapeDtypeStruct((G, S5_ROWS, TH), bf16),
                   jax.ShapeDtypeStruct((G, BATCH, 2 * P2), f32)),
        grid=(G,),
        in_specs=[
            pl.BlockSpec((1, S5_ROWS, TH), lambda g: (g, 0, 0)),
            pl.BlockSpec((1, TH, TH), lambda g: (g, 0, 0)),
            pl.BlockSpec((1, TH, 4 * P2), lambda g: (g, 0, 0)),
            pl.BlockSpec((1, 2 * P2, TH), lambda g: (g, 0, 0)),
            pl.BlockSpec((1, 8, P2), lambda g: (g, 0, 0)),
            pl.BlockSpec((1, DEC_BATCH, 4 * P2), lambda g: (g, 0, 0)),
        ],
        out_specs=(pl.BlockSpec((1, S5_ROWS, TH), lambda g: (g, 0, 0)),
                   pl.BlockSpec((1, BATCH, 2 * P2), lambda g: (g, 0, 0))),
        scratch_shapes=[pltpu.VMEM((S5_ROWS, 4 * P2), f32), pltpu.VMEM((S5_ROWS, 2 * P2), f32)],
        compiler_params=_cparams(("arbitrary",)),
        name="s5_scan",
    )(u_g, toeplitz, w_state, w_carry, coef, h0)


def _s5_to_groups(u):
    G, H, T = S5_GROUPS, S5_GROUP, S5_T
    ctx = u[:N_CTX].reshape(BATCH, S5_CTX_CHUNKS, T, G, H)
    lat = u[N_CTX:].reshape(DEC_BATCH, S5_LAT_CHUNKS, T, G, H)
    ctx = jnp.transpose(ctx, (3, 1, 0, 2, 4)).reshape(G, S5_CTX_CHUNKS * BATCH, T * H)
    lat = jnp.transpose(lat, (3, 1, 0, 2, 4)).reshape(G, S5_LAT_CHUNKS * DEC_BATCH, T * H)
    return jnp.concatenate([ctx, lat], axis=1)


def _s5_from_groups(y):
    G, H, T = S5_GROUPS, S5_GROUP, S5_T
    nc = S5_CTX_CHUNKS * BATCH
    ctx = y[:, :nc].reshape(G, S5_CTX_CHUNKS, BATCH, T, H)
    lat = y[:, nc:].reshape(G, S5_LAT_CHUNKS, DEC_BATCH, T, H)
    ctx = jnp.transpose(ctx, (2, 1, 3, 0, 4)).reshape(N_CTX, G * H)
    lat = jnp.transpose(lat, (2, 1, 3, 0, 4)).reshape(N_LAT, G * H)
    return jnp.concatenate([ctx, lat], axis=0)


def _s5_init_rows(s_re, s_im):
    parts = [s_re[:, 0], s_im[:, 0], s_im[:, 0], s_re[:, 0], s_re[:, 1], s_im[:, 1], s_im[:, 1], s_re[:, 1]]
    return jnp.transpose(jnp.concatenate(parts, axis=-1), (1, 0, 2))


def _gla_seq_of_chunk(cj):
    per_ctx = SEQ // GLA_C
    per_lat = DEC_SEQ // GLA_C
    return jnp.where(cj < GLA_CTX_CHUNKS, cj // per_ctx, BATCH + (cj - GLA_CTX_CHUNKS) // per_lat)


def _gla_kernel(q_ref, k_ref, v_ref, lg_ref, s0_ref, o_ref, fin_ref, st_ref, a_ref, *, reverse):
    j = pl.program_id(0)
    cj = (GLA_CHUNKS - 1 - j) if reverse else j
    per_ctx = SEQ // GLA_C
    per_lat = DEC_SEQ // GLA_C
    pos = jnp.where(cj < GLA_CTX_CHUNKS, cj % per_ctx, (cj - GLA_CTX_CHUNKS) % per_lat)
    n_in_seq = jnp.where(cj < GLA_CTX_CHUNKS, per_ctx, per_lat)
    is_first = (pos == n_in_seq - 1) if reverse else (pos == 0)
    is_last = (pos == 0) if reverse else (pos == n_in_seq - 1)
    is_ctx = cj < GLA_CTX_CHUNKS

    @pl.when(is_first & is_ctx)
    def _():
        st_ref[...] = jnp.zeros(st_ref.shape, f32)

    @pl.when(is_first & jnp.logical_not(is_ctx))
    def _():
        st_ref[...] = s0_ref[0, 0, 0]

    C, SUB = GLA_C, GLA_SUB
    row = lax.broadcasted_iota(i32, (C, C), 0)
    col = lax.broadcasted_iota(i32, (C, C), 1)
    if reverse:
        m_cum = (col >= row)
        m_ref = (col >= (row // SUB + 1) * SUB)
    else:
        m_cum = (col <= row)
        m_ref = (col < (row // SUB) * SUB)
    m_both = jnp.concatenate([m_cum.astype(f32), m_ref.astype(f32)], axis=0)
    lg_all = lg_ref[...]
    sums = jnp.dot(m_both, lg_all, precision=HIGHEST, preferred_element_type=f32)
    scale = GLA_DK ** -0.5
    a_ref[...] = jnp.zeros((C, C), f32)

    for h in range(GLA_HEADS):
        ks = slice(h * GLA_DK, (h + 1) * GLA_DK)
        vs = slice(h * GLA_DV, (h + 1) * GLA_DV)
        lg = lg_all[:, ks]
        cum = sums[0:C, ks]
        ref = sums[C:2 * C, ks]
        tot = jnp.sum(lg, axis=0, keepdims=True)
        q = q_ref[:, ks].astype(f32) * scale
        k = k_ref[:, ks].astype(f32)
        v = v_ref[:, vs]
        state = st_ref[h]

        q_in = (q * jnp.exp(cum)).astype(bf16)
        o = jnp.dot(q_in, state.astype(bf16), preferred_element_type=f32)
        k_st = k * jnp.exp(tot - cum)
        tot_col = jnp.sum(lg.T, axis=1, keepdims=True)
        st_ref[h] = jnp.exp(tot_col) * state + jnp.dot(k_st.T.astype(bf16), v, preferred_element_type=f32)

        qt = (q * jnp.exp(cum - ref)).astype(bf16)
        for blk in range(GLA_NSUB):
            rows = slice(blk * SUB, (blk + 1) * SUB)
            cols = slice((blk + 1) * SUB, C) if reverse else slice(0, blk * SUB)
            if cols.start == cols.stop:
                continue
            kt = (k[cols] * jnp.exp(ref[blk * SUB:blk * SUB + 1] - cum[cols])).astype(bf16)
            a_ref[rows, cols] = lax.dot_general(qt[rows], kt, (((1,), (1,)), ((), ())), preferred_element_type=f32)

        diag = jnp.zeros((C, C), f32)
        for lag in range(SUB):
            shift = (C - lag) % C if reverse else lag
            k_sh = k if lag == 0 else pltpu.roll(k, shift, axis=0)
            c_sh = cum if lag == 0 else pltpu.roll(cum, shift, axis=0)
            z = q * k_sh * jnp.exp(jnp.minimum(cum - c_sh, 0.0))
            r = jnp.sum(z, axis=1, keepdims=True)
            if reverse:
                hit = (col == row + lag) & (row % SUB + lag < SUB)
            else:
                hit = (col == row - lag) & (row % SUB >= lag)
            diag = diag + jnp.where(hit, r, 0.0)
        scores = (a_ref[...] + diag).astype(bf16)
        o_ref[:, vs] = o + jnp.dot(scores, v, preferred_element_type=f32)

    @pl.when(is_last & is_ctx)
    def _():
        fin_ref[0] = st_ref[...]


def _gla_direction(proj, logg, state_gla, layer, reverse):
    dirn = 1 if reverse else 0
    seq = lambda j: _gla_seq_of_chunk(cidx(j))
    cidx = (lambda j: GLA_CHUNKS - 1 - j) if reverse else (lambda j: j)
    q_blk = (S5_WIDTH) // GLA_QK
    v_blk = (S5_WIDTH + 2 * GLA_QK) // GLA_V
    return pl.pallas_call(
        functools.partial(_gla_kernel, reverse=reverse),
        out_shape=(jax.ShapeDtypeStruct((NT, GLA_V), f32),
                   jax.ShapeDtypeStruct((BATCH, GLA_HEADS, GLA_DK, GLA_DV), f32)),
        grid=(GLA_CHUNKS,),
        in_specs=[
            pl.BlockSpec((GLA_C, GLA_QK), lambda j: (cidx(j), q_blk)),
            pl.BlockSpec((GLA_C, GLA_QK), lambda j: (cidx(j), q_blk + 1)),
            pl.BlockSpec((GLA_C, GLA_V), lambda j: (cidx(j), v_blk)),
            pl.BlockSpec((GLA_C, GLA_QK), lambda j: (cidx(j), dirn)),
            pl.BlockSpec((1, 1, 1, GLA_HEADS, GLA_DK, GLA_DV),
                         lambda j: (jnp.maximum(seq(j) - BATCH, 0), layer, dirn, 0, 0, 0)),
        ],
        out_specs=(pl.BlockSpec((GLA_C, GLA_V), lambda j: (cidx(j), 0)),
                   pl.BlockSpec((1, GLA_HEADS, GLA_DK, GLA_DV), lambda j: (jnp.minimum(seq(j), BATCH - 1), 0, 0, 0))),
        scratch_shapes=[pltpu.VMEM((GLA_HEADS, GLA_DK, GLA_DV), f32), pltpu.VMEM((GLA_C, GLA_C), f32)],
        compiler_params=_cparams(("arbitrary",)),
        name="gla_bwd" if reverse else "gla_fwd",
    )(proj, proj, proj, logg, state_gla)


def _post_kernel(y_ref, of_ref, ob_ref, r_ref, ga_ref, gb_ref, x_ref, mod_ref,
                 wglu_ref, bglu_ref, ws5_ref, gn_ref, wgla_ref, wout_ref, n2_ref, wr_ref, br_ref,
                 x1_ref, h2_ref, gate_ref, idx_ref):
    g = jax.nn.gelu(y_ref[...].astype(f32))
    glu = jnp.dot(g.astype(bf16), wglu_ref[...], preferred_element_type=f32) + bglu_ref[...]
    ya = g * jax.nn.sigmoid(glu)

    o = of_ref[...] + ob_ref[...]
    heads = []
    for h in range(GLA_HEADS):
        oh = o[:, h * GLA_DV:(h + 1) * GLA_DV]
        heads.append(oh * lax.rsqrt(jnp.mean(oh * oh, axis=-1, keepdims=True) + EPS))
    r = r_ref[...].astype(f32)
    yb = jnp.concatenate(heads, axis=1) * gn_ref[...] * (r * jax.nn.sigmoid(r))

    merged = (jax.nn.sigmoid(ga_ref[...].astype(f32)) * jnp.dot(ya.astype(bf16), ws5_ref[...], preferred_element_type=f32)
              + jax.nn.sigmoid(gb_ref[...].astype(f32)) * jnp.dot(yb.astype(bf16), wgla_ref[...], preferred_element_type=f32))
    x1 = x_ref[...] + mod_ref[0, 2:3, :] * jnp.dot(merged.astype(bf16), wout_ref[...], preferred_element_type=f32)
    x1_ref[...] = x1

    y2 = x1 * lax.rsqrt(jnp.mean(x1 * x1, axis=-1, keepdims=True) + EPS) * n2_ref[...]
    h2 = y2 * (1.0 + mod_ref[0, 4:5, :]) + mod_ref[0, 3:4, :]
    h2_ref[...] = h2

    logits = jnp.dot(h2, wr_ref[...], precision=HIGHEST, preferred_element_type=f32) + br_ref[...]
    lane = lax.broadcasted_iota(i32, logits.shape, 1)
    lane_f = lane.astype(f32)
    vals = logits
    top_v, top_i = [], []
    for _ in range(TOP_K):
        m = jnp.max(vals, axis=-1, keepdims=True)
        idx = jnp.min(jnp.where(vals == m, lane_f, float(LANES)), axis=-1, keepdims=True).astype(i32)
        top_v.append(m)
        top_i.append(idx)
        vals = jnp.where(lane == idx, -jnp.inf, vals)
    ex = [jnp.exp(v - top_v[0]) for v in top_v]
    inv = 1.0 / (ex[0] + ex[1] + ex[2] + ex[3])
    gates = jnp.zeros(logits.shape, f32)
    idxs = jnp.zeros(logits.shape, i32)
    for kk in range(TOP_K):
        gates = jnp.where(lane == kk, ex[kk] * inv, gates)
        idxs = jnp.where(lane == kk, top_i[kk], idxs)
    gate_ref[...] = gates
    idx_ref[...] = idxs


def _post(y_s5, o_f, o_b, proj, x, mod_l, wglu, bglu, ws5, gnorm, wgla, wout, norm2, wr, br):
    const = lambda shape: pl.BlockSpec(shape, lambda i: (0,) * len(shape), pipeline_mode=pl.Buffered(1))
    r_blk = (S5_WIDTH + 2 * GLA_QK + GLA_V) // GLA_V
    ga_blk = 4096 // D_MODEL
    return pl.pallas_call(
        _post_kernel,
        out_shape=(jax.ShapeDtypeStruct((NT, D_MODEL), f32),
                   jax.ShapeDtypeStruct((NT, D_MODEL), f32),
                   jax.ShapeDtypeStruct((NT, LANES), f32),
                   jax.ShapeDtypeStruct((NT, LANES), i32)),
        grid=(NT // POST_TM,),
        in_specs=[
            pl.BlockSpec((POST_TM, S5_WIDTH), lambda i: (i, 0)),
            pl.BlockSpec((POST_TM, GLA_V), lambda i: (i, 0)),
            pl.BlockSpec((POST_TM, GLA_V), lambda i: (i, 0)),
            pl.BlockSpec((POST_TM, GLA_V), lambda i: (i, r_blk)),
            pl.BlockSpec((POST_TM, D_MODEL), lambda i: (i, ga_blk)),
            pl.BlockSpec((POST_TM, D_MODEL), lambda i: (i, ga_blk + 1)),
            pl.BlockSpec((POST_TM, D_MODEL), lambda i: (i, 0)),
            pl.BlockSpec((1, 6, D_MODEL), lambda i: (_row_seq_of_tile(i, POST_TM), 0, 0)),
            const((S5_WIDTH, S5_WIDTH)), const((1, S5_WIDTH)), const((S5_WIDTH, D_MODEL)),
            const((1, GLA_V)), const((GLA_V, D_MODEL)), const((D_MODEL, D_MODEL)), const((1, D_MODEL)),
            const((D_MODEL, LANES)), const((1, LANES)),
        ],
        out_specs=(pl.BlockSpec((POST_TM, D_MODEL), lambda i: (i, 0)),
                   pl.BlockSpec((POST_TM, D_MODEL), lambda i: (i, 0)),
                   pl.BlockSpec((POST_TM, LANES), lambda i: (i, 0)),
                   pl.BlockSpec((POST_TM, LANES), lambda i: (i, 0))),
        compiler_params=_cparams(("arbitrary",)),
        name="post",
    )(y_s5, o_f, o_b, proj, proj, proj, x, mod_l, wglu, bglu, ws5, gnorm, wgla, wout, norm2, wr, br)


def _routing(top_idx):
    m = NT * TOP_K
    e_flat = top_idx.reshape(-1)
    experts = jnp.arange(N_EXPERTS, dtype=i32)
    onehot = (e_flat[:, None] == experts[None, :]).astype(i32)
    csum = jnp.cumsum(onehot, axis=0)
    counts = csum[-1]
    starts = jnp.cumsum(counts) - counts
    nblk = (counts + MOE_BM - 1) // MOE_BM
    blk_end = jnp.cumsum(nblk)
    blk_start = blk_end - nblk
    dest = jnp.sum(onehot * (blk_start[None, :] * MOE_BM + csum - 1), axis=1).astype(i32)

    order = jnp.argsort(e_flat, stable=True).astype(i32)
    p = jnp.arange(MOE_ROWS, dtype=i32)
    e_p = jnp.minimum(jnp.sum((blk_end[None, :] <= (p // MOE_BM)[:, None]).astype(i32), axis=1), N_EXPERTS - 1)
    oh_p = (e_p[:, None] == experts[None, :]).astype(i32)
    pick = lambda v: jnp.sum(oh_p * v[None, :], axis=1)
    idx_in = p - pick(blk_start) * MOE_BM
    valid = (p < blk_end[-1] * MOE_BM) & (idx_in < pick(counts))
    src = order[jnp.clip(pick(starts) + idx_in, 0, m - 1)] // TOP_K
    row_tok = jnp.where(valid, src, 0).astype(i32)

    n_units = (nblk + MOE_RB - 1) // MOE_RB
    unit_end = jnp.cumsum(n_units)
    unit_start = unit_end - n_units
    total_units = unit_end[-1]
    u = jnp.arange(MOE_UNITS, dtype=i32)
    used = u < total_units
    e_of_u = jnp.minimum(jnp.sum((unit_end[None, :] <= u[:, None]).astype(i32), axis=1), N_EXPERTS - 1)
    local = u - unit_start[e_of_u]
    u_blk0 = blk_start[e_of_u] + local * MOE_RB
    u_nblk = jnp.minimum(MOE_RB, nblk[e_of_u] - local * MOE_RB)
    last_e = e_of_u[jnp.maximum(total_units - 1, 0)]
    u_exp = jnp.where(used, e_of_u, last_e).astype(i32)
    u_blk0 = jnp.where(used, u_blk0, 0).astype(i32)
    u_nblk = jnp.where(used, u_nblk, 0).astype(i32)
    return row_tok, dest, u_exp, u_blk0, u_nblk, blk_end[-1:].astype(i32)


def _gather_kernel(tok_ref, src_ref, o_ref, stage_ref, sem):
    i = pl.program_id(0)
    S = GATHER_SPLIT

    def row_copy(t, slot, r):
        return pltpu.make_async_copy(src_ref.at[pl.ds(pl.multiple_of(t * S, S), S), :],
                                     stage_ref.at[slot, pl.ds(pl.multiple_of(r * S, S), S), :], sem.at[slot])

    def issue(step, slot):
        base = step * GATHER_GB
        lax.fori_loop(0, GATHER_GB, lambda r, c: (row_copy(tok_ref[base + r], slot, r).start(), c)[1], 0)

    @pl.when(i == 0)
    def _():
        issue(0, 0)

    @pl.when(i + 1 < pl.num_programs(0))
    def _():
        issue(i + 1, (i + 1) % 2)

    def finish(slot):
        lax.fori_loop(0, GATHER_GB, lambda r, c: (row_copy(0, slot, r).wait(), c)[1], 0)
        parts = [stage_ref[slot, pl.ds(k, GATHER_GB, stride=S), :] for k in range(S)]
        o_ref[...] = jnp.concatenate(parts, axis=1).astype(bf16)

    for slot in range(2):
        pl.when(i % 2 == slot)(functools.partial(finish, slot))


def _gather_rows(h2, row_tok):
    S = GATHER_SPLIT
    return pl.pallas_call(
        _gather_kernel,
        out_shape=jax.ShapeDtypeStruct((MOE_ROWS, D_MODEL), bf16),
        grid_spec=pltpu.PrefetchScalarGridSpec(
            num_scalar_prefetch=1,
            grid=(MOE_ROWS // GATHER_GB,),
            in_specs=[pl.BlockSpec(memory_space=pl.ANY)],
            out_specs=pl.BlockSpec((GATHER_GB, D_MODEL), lambda i, tok: (i, 0)),
            scratch_shapes=[pltpu.VMEM((2, GATHER_GB * S, D_MODEL // S), f32), pltpu.SemaphoreType.DMA((2,))],
        ),
        compiler_params=_cparams(("arbitrary",)),
        name="moe_gather",
    )(row_tok, h2.reshape(NT * S, D_MODEL // S))


def _deinterleave_matrix():
    half = MOE_TA // 2
    src = jnp.arange(MOE_TA)[:, None]
    dst = jnp.arange(MOE_TA)[None, :]
    return (dst == (src % 2) * half + src // 2).astype(bf16)


def _moe_kernel(uexp_ref, ublk_ref, unb_ref, nused_ref, x_hbm, w1_ref, b1_ref, perm_ref, w2_ref, b2_ref,
                y_hbm, x_buf, act_buf, w_buf, y_buf, zero_buf, sem_in, sem_out):
    u = pl.program_id(0)
    j = pl.program_id(1)
    nb = unb_ref[u]
    row0 = ublk_ref[u] * MOE_BM
    big = MOE_QB * MOE_BM
    n_big = nb // MOE_QB
    n_small = nb - n_big * MOE_QB
    small0 = n_big * big

    def in_copy(r):
        rows = pl.ds(pl.multiple_of(r * MOE_BM, MOE_BM), MOE_BM)
        src = x_hbm.at[pl.ds(pl.multiple_of(row0 + r * MOE_BM, MOE_BM), MOE_BM), :]
        return pltpu.make_async_copy(src, x_buf.at[rows, :], sem_in)

    @pl.when((j == 0) & (nb > 0))
    def _():
        lax.fori_loop(0, nb, lambda r, c: (in_copy(r).start(), c)[1], 0)
        lax.fori_loop(0, nb, lambda r, c: (in_copy(r).wait(), c)[1], 0)

    @pl.when((j < MOE_J1) & (nb > 0))
    def _():
        w_buf[...] = w1_ref[0, 0].astype(bf16)
        half = MOE_TA // 2

        def act_rows(start, m):
            rows = pl.ds(pl.multiple_of(start, MOE_BM), m)
            h = jnp.dot(x_buf[rows, :], w_buf[...], preferred_element_type=f32) + b1_ref[0, 0]
            hb = h.astype(bf16)
            parts = []
            for s in range(MOE_TN // MOE_TA):
                hp = jnp.dot(hb[:, s * MOE_TA:(s + 1) * MOE_TA], perm_ref[...], preferred_element_type=f32)
                hg = jnp.minimum(hp[:, :half], SWIGLU_LIMIT)
                hl = jnp.clip(hp[:, half:], -SWIGLU_LIMIT, SWIGLU_LIMIT)
                parts.append(hg * jax.nn.sigmoid(SWIGLU_ALPHA * hg) * (hl + 1.0))
            act_buf[j, rows, :] = jnp.concatenate(parts, axis=1).astype(bf16)

        lax.fori_loop(0, n_big, lambda q, c: (act_rows(q * big, big), c)[1], 0)
        lax.fori_loop(0, n_small, lambda r, c: (act_rows(small0 + r * MOE_BM, MOE_BM), c)[1], 0)

    @pl.when((j >= MOE_J1) & (nb > 0))
    def _():
        w_buf[...] = w2_ref[0, 0].astype(bf16)
        col0 = pl.multiple_of((j - MOE_J1) * MOE_TN, MOE_TN)

        def out_copy(start, m):
            rows = pl.ds(pl.multiple_of(start, MOE_BM), m)
            dst = y_hbm.at[pl.ds(pl.multiple_of(row0 + start, MOE_BM), m), pl.ds(col0, MOE_TN)]
            return pltpu.make_async_copy(y_buf.at[rows, :], dst, sem_out)

        def out_rows(start, m):
            rows = pl.ds(pl.multiple_of(start, MOE_BM), m)
            acc = b2_ref[0, 0] + jnp.zeros((m, MOE_TN), f32)
            for jj in range(MOE_J1):
                acc = acc + jnp.dot(act_buf[jj, rows, :], w_buf[jj * MOE_TA:(jj + 1) * MOE_TA, :],
                                    preferred_element_type=f32)
            y_buf[rows, :] = acc
            out_copy(start, m).start()

        lax.fori_loop(0, n_big, lambda q, c: (out_rows(q * big, big), c)[1], 0)
        lax.fori_loop(0, n_small, lambda r, c: (out_rows(small0 + r * MOE_BM, MOE_BM), c)[1], 0)
        lax.fori_loop(0, n_big, lambda q, c: (out_copy(q * big, big).wait(), c)[1], 0)
        lax.fori_loop(0, n_small, lambda r, c: (out_copy(small0 + r * MOE_BM, MOE_BM).wait(), c)[1], 0)

    @pl.when((u == MOE_UNITS - 1) & (j == MOE_J1 + MOE_J2 - 1))
    def _():
        zero_buf[...] = jnp.zeros(zero_buf.shape, f32)

        def fill_copy(r):
            dst = y_hbm.at[pl.ds(pl.multiple_of(r * MOE_BM, MOE_BM), MOE_BM), :]
            return pltpu.make_async_copy(zero_buf, dst, sem_out)

        lax.fori_loop(nused_ref[0], MOE_BLOCKS, lambda r, c: (fill_copy(r).start(), c)[1], 0)
        lax.fori_loop(nused_ref[0], MOE_BLOCKS, lambda r, c: (fill_copy(r).wait(), c)[1], 0)


def _moe_experts(x_sorted, u_exp, u_blk0, u_nblk, n_used, w1, b1, w2, b2, layer):
    def w1_idx(u, j, uexp, ublk, unb, nused):
        return (layer, uexp[u], 0, jnp.where(unb[u] > 0, jnp.minimum(j, MOE_J1 - 1), MOE_J1 - 1))

    def w2_idx(u, j, uexp, ublk, unb, nused):
        return (layer, uexp[u], 0, jnp.where(unb[u] > 0, jnp.maximum(j - MOE_J1, 0), MOE_J2 - 1))

    return pl.pallas_call(
        _moe_kernel,
        out_shape=jax.ShapeDtypeStruct((MOE_ROWS, D_MODEL), f32),
        grid_spec=pltpu.PrefetchScalarGridSpec(
            num_scalar_prefetch=4,
            grid=(MOE_UNITS, MOE_J1 + MOE_J2),
            in_specs=[
                pl.BlockSpec(memory_space=pl.ANY),
                pl.BlockSpec((1, 1, D_MODEL, MOE_TN), w1_idx),
                pl.BlockSpec((1, 1, 1, MOE_TN), w1_idx),
                pl.BlockSpec((MOE_TA, MOE_TA), lambda u, j, *_: (0, 0)),
                pl.BlockSpec((1, 1, D_MODEL, MOE_TN), w2_idx),
                pl.BlockSpec((1, 1, 1, MOE_TN), w2_idx),
            ],
            out_specs=pl.BlockSpec(memory_space=pl.ANY),
            scratch_shapes=[
                pltpu.VMEM((MOE_RMAX, D_MODEL), bf16),
                pltpu.VMEM((MOE_J1, MOE_RMAX, MOE_TA), bf16),
                pltpu.VMEM((D_MODEL, MOE_TN), bf16),
                pltpu.VMEM((MOE_RMAX, MOE_TN), f32),
                pltpu.VMEM((MOE_BM, D_MODEL), f32),
                pltpu.SemaphoreType.DMA(()),
                pltpu.SemaphoreType.DMA(()),
            ],
        ),
        compiler_params=_cparams(("arbitrary", "arbitrary")),
        name="moe_experts",
    )(u_exp, u_blk0, u_nblk, n_used, x_sorted, w1, b1, _deinterleave_matrix(), w2, b2)


def _combine_kernel(dest_ref, y_hbm, gate_ref, x1_ref, mod_ref, nf_ref, o_ref, y_buf, sem, *, final_norm):
    i = pl.program_id(0)

    def row_copy(p, slot, kk, r):
        return pltpu.make_async_copy(y_hbm.at[pl.ds(p, 1), :], y_buf.at[slot, kk, pl.ds(r, 1), :], sem.at[slot])

    def issue(step, slot):
        base = step * COMB_TM * TOP_K

        def body(r, c):
            for kk in range(TOP_K):
                row_copy(dest_ref[base + r * TOP_K + kk], slot, kk, r).start()
            return c

        lax.fori_loop(0, COMB_TM, body, 0)

    @pl.when(i == 0)
    def _():
        issue(0, 0)

    @pl.when(i + 1 < pl.num_programs(0))
    def _():
        issue(i + 1, (i + 1) % 2)

    def finish(slot):
        def wait(r, c):
            for kk in range(TOP_K):
                row_copy(0, slot, kk, r).wait()
            return c

        lax.fori_loop(0, COMB_TM, wait, 0)
        gates = gate_ref[...]
        acc = jnp.zeros((COMB_TM, D_MODEL), f32)
        for kk in range(TOP_K):
            acc = acc + gates[:, kk:kk + 1] * y_buf[slot, kk]
        x2 = x1_ref[...] + mod_ref[0, 5:6, :] * acc
        if final_norm:
            x2 = x2 * lax.rsqrt(jnp.mean(x2 * x2, axis=-1, keepdims=True) + EPS) * nf_ref[...]
        o_ref[...] = x2

    for slot in range(2):
        pl.when(i % 2 == slot)(functools.partial(finish, slot))


def _combine(y_sorted, dest, gates, x1, mod_l, norm_f, final_norm):
    return pl.pallas_call(
        functools.partial(_combine_kernel, final_norm=final_norm),
        out_shape=jax.ShapeDtypeStruct((NT, D_MODEL), f32),
        grid_spec=pltpu.PrefetchScalarGridSpec(
            num_scalar_prefetch=1,
            grid=(NT // COMB_TM,),
            in_specs=[
                pl.BlockSpec(memory_space=pl.ANY),
                pl.BlockSpec((COMB_TM, LANES), lambda i, d: (i, 0)),
                pl.BlockSpec((COMB_TM, D_MODEL), lambda i, d: (i, 0)),
                pl.BlockSpec((1, 6, D_MODEL), lambda i, d: (_row_seq_of_tile(i, COMB_TM), 0, 0)),
                pl.BlockSpec((1, D_MODEL), lambda i, d: (0, 0)),
            ],
            out_specs=pl.BlockSpec((COMB_TM, D_MODEL), lambda i, d: (i, 0)),
            scratch_shapes=[pltpu.VMEM((2, TOP_K, COMB_TM, D_MODEL), f32), pltpu.SemaphoreType.DMA((2,))],
        ),
        compiler_params=_cparams(("arbitrary",)),
        name="moe_combine_final" if final_norm else "moe_combine",
    )(dest, y_sorted, gates, x1, mod_l, norm_f)


def kernel(x_prompt, x_sample, c, state_s5_re, state_s5_im, state_gla, c_ctx, w_mod, b_mod, norm1, w_in, s5_lam_re, s5_lam_im, s5_log_dt, s5_b_re, s5_b_im, s5_c_re, s5_c_im, s5_d, s5_w_glu, s5_b_glu, w_s5_out, gla_w_g2, gla_b_g, gla_norm, w_gla_out, w_out, norm2, w_router, b_router, w_e1, b_e1, w_e2, b_e2, norm_f):
    cond = jnp.concatenate([c_ctx[None], c, jnp.zeros((N_MODROWS - 1 - DEC_BATCH, D_MODEL), f32)], axis=0)
    mod = _modulation(cond, w_mod, b_mod).reshape(DEPTH, N_MODROWS, 6, D_MODEL)
    x = _embed(x_prompt, x_sample, _grid_pos_embed())

    glr0 = S5_WIDTH + 2 * GLA_QK + 2 * GLA_V
    glr1 = glr0 + N_DIRS * GLA_RANK
    s5_re_out, s5_im_out, gla_out = [], [], []
    toeplitz, w_state, w_carry, coef = jax.vmap(_s5_weights)(
        s5_lam_re, s5_lam_im, s5_log_dt, s5_b_re, s5_b_im, s5_c_re, s5_c_im, s5_d)
    b_e1_r = b_e1.reshape(DEPTH, N_EXPERTS, 1, 2 * D_MODEL)
    b_e2_r = b_e2.reshape(DEPTH, N_EXPERTS, 1, D_MODEL)
    for l in range(DEPTH):
        mod_l = mod[l]
        w_main = jnp.concatenate([w_in[l, :, :glr0], w_in[l, :, glr1:]], axis=1).astype(bf16)
        w_glr = w_in[l, :, glr0:glr1].astype(bf16)
        zg = jnp.zeros((GLA_RANK, GLA_QK), f32)
        w_g2bd = jnp.concatenate([jnp.concatenate([gla_w_g2[l, 0], zg], axis=1),
                                  jnp.concatenate([zg, gla_w_g2[l, 1]], axis=1)], axis=0)
        b_g = gla_b_g[l].reshape(1, N_DIRS * GLA_QK)
        h0 = _s5_init_rows(state_s5_re[:, l], state_s5_im[:, l])
        w_r = jnp.concatenate([w_router[l], jnp.zeros((D_MODEL, LANES - N_EXPERTS), f32)], axis=1)
        b_r = jnp.concatenate([b_router[l], jnp.full((LANES - N_EXPERTS,), -jnp.inf, f32)]).reshape(1, LANES)

        proj, logg = _pre(x, mod_l, norm1[l].reshape(1, D_MODEL), w_main, w_glr, w_g2bd, b_g)
        y_g, s5_fin = _s5_scan(_s5_to_groups(proj[:, :S5_WIDTH]), toeplitz[l], w_state[l], w_carry[l], coef[l], h0)
        y_s5 = _s5_from_groups(y_g)
        o_f, st_f = _gla_direction(proj, logg, state_gla, l, False)
        o_b, st_b = _gla_direction(proj, logg, state_gla, l, True)

        x1, h2, gates, top_idx = _post(
            y_s5, o_f, o_b, proj, x, mod_l,
            s5_w_glu[l].astype(bf16), s5_b_glu[l].reshape(1, S5_WIDTH), w_s5_out[l].astype(bf16),
            jnp.tile(gla_norm[l], GLA_HEADS).reshape(1, GLA_V), w_gla_out[l].astype(bf16), w_out[l].astype(bf16),
            norm2[l].reshape(1, D_MODEL), w_r, b_r)

        row_tok, dest, u_exp, u_blk0, u_nblk, n_used = _routing(top_idx[:, :TOP_K])
        x_sorted = _gather_rows(h2, row_tok)
        y_sorted = _moe_experts(x_sorted, u_exp, u_blk0, u_nblk, n_used, w_e1, b_e1_r, w_e2, b_e2_r, l)
        x = _combine(y_sorted, dest, gates, x1, mod_l, norm_f.reshape(1, D_MODEL), l == DEPTH - 1)

        P = S5_STATE
        fin = jnp.transpose(s5_fin, (1, 0, 2))
        s5_re_out.append(jnp.stack([fin[..., 0:P], fin[..., 2 * P:3 * P]], axis=1))
        s5_im_out.append(jnp.stack([fin[..., P:2 * P], fin[..., 3 * P:4 * P]], axis=1))
        gla_out.append(jnp.stack([st_f, st_b], axis=1))

    y_prompt = x[:N_CTX].reshape(BATCH, SEQ, D_MODEL)
    y_sample = x[N_CTX:].reshape(DEC_BATCH, DEC_SEQ, D_MODEL)
    return (y_prompt, y_sample, jnp.stack(s5_re_out, axis=1), jnp.stack(s5_im_out, axis=1),
            jnp.stack(gla_out, axis=1))
```

```python
import functools
import math

import jax
import jax.numpy as jnp
import numpy as np
from jax import lax
from jax.experimental import pallas as pl
from jax.experimental.pallas import tpu as pltpu

f32 = jnp.float32
bf16 = jnp.bfloat16
i32 = jnp.int32
HIGHEST = lax.Precision.HIGHEST

D_MODEL = 2048
BATCH = 16
SEQ = 256
DEPTH = 2
DEC_BATCH = 2
DEC_SEQ = 2048
GRID_W = 64
N_DIRS = 2
S5_WIDTH = 1024
S5_GROUP = 16
S5_GROUPS = 64
S5_STATE = 64
GLA_HEADS = 4
GLA_DK = 128
GLA_DV = 256
GLA_QK = 512
GLA_V = 1024
GLA_RANK = 16
GLA_GATE_NORM = 16.0
N_EXPERTS = 32
TOP_K = 4
SWIGLU_LIMIT = 7.0
SWIGLU_ALPHA = 1.702
POS_BASE = 10000.0
EPS = 1e-6

N_CTX = BATCH * SEQ
N_LAT = DEC_BATCH * DEC_SEQ
NT = N_CTX + N_LAT
N_SEQS = BATCH + DEC_BATCH
N_MODROWS = 8

LANES = 128
VMEM_LIMIT = 56 * 1024 * 1024

S5_T = 16
S5_ROWS = NT // S5_T
S5_CTX_CHUNKS = SEQ // S5_T
S5_LAT_CHUNKS = DEC_SEQ // S5_T
GLA_C = 128
GLA_SUB = 16
GLA_NSUB = GLA_C // GLA_SUB
GLA_CHUNKS = NT // GLA_C
GLA_CTX_CHUNKS = N_CTX // GLA_C
PRE_TM = 1024
PRE_TN = 1024
PROJ_COLS = 8192
POST_TM = 256
MOE_BM = 128
MOE_ROWS = NT * TOP_K + N_EXPERTS * MOE_BM
MOE_BLOCKS = MOE_ROWS // MOE_BM
MOE_RB = 16
MOE_QB = 4
MOE_RMAX = MOE_RB * MOE_BM
MOE_UNITS = N_EXPERTS + MOE_BLOCKS // MOE_RB
MOE_TN = 512
MOE_TA = MOE_TN // 2
MOE_J1 = 2 * D_MODEL // MOE_TN
MOE_J2 = D_MODEL // MOE_TN
GATHER_GB = 512
GATHER_SPLIT = D_MODEL // LANES
COMB_TM = 128


def _cparams(sem, **kw):
    return pltpu.CompilerParams(dimension_semantics=sem, vmem_limit_bytes=VMEM_LIMIT, **kw)


def _row_seq_of_tile(i, tile_rows):
    first = i * tile_rows
    return jnp.where(first < N_CTX, 0, 1 + (first - N_CTX) // DEC_SEQ)


def _log_sigmoid(x):
    return -(jnp.maximum(-x, 0.0) + jnp.log1p(jnp.exp(-jnp.abs(x))))


MOD_TN = 1024


def _mod_kernel(c_ref, w_ref, b_ref, o_ref):
    c = c_ref[...]
    s = (c * jax.nn.sigmoid(c)).astype(bf16)
    o_ref[0] = jnp.dot(s, w_ref[0].astype(bf16), preferred_element_type=f32) + b_ref[0]


def _modulation(cond, w_mod, b_mod):
    n_out = 6 * D_MODEL
    return pl.pallas_call(
        _mod_kernel,
        out_shape=jax.ShapeDtypeStruct((DEPTH, N_MODROWS, n_out), f32),
        grid=(DEPTH, n_out // MOD_TN),
        in_specs=[
            pl.BlockSpec((N_MODROWS, D_MODEL), lambda l, j: (0, 0)),
            pl.BlockSpec((1, D_MODEL, MOD_TN), lambda l, j: (l, 0, j)),
            pl.BlockSpec((1, 1, MOD_TN), lambda l, j: (l, 0, j)),
        ],
        out_specs=pl.BlockSpec((1, N_MODROWS, MOD_TN), lambda l, j: (l, 0, j)),
        compiler_params=_cparams(("arbitrary", "arbitrary")),
        name="modulation",
    )(cond, w_mod, b_mod.reshape(DEPTH, 1, n_out))


EMB_TM = 512


def _embed_kernel(xp_ref, xs_ref, pe_ref, o_ref):
    i = pl.program_id(0)

    @pl.when(i < N_CTX // EMB_TM)
    def _():
        o_ref[...] = xp_ref[...]

    @pl.when(i >= N_CTX // EMB_TM)
    def _():
        o_ref[...] = xs_ref[...] + pe_ref[...]


def _embed(x_prompt, x_sample, pe):
    nc = N_CTX // EMB_TM
    per_seq = DEC_SEQ // EMB_TM
    return pl.pallas_call(
        _embed_kernel,
        out_shape=jax.ShapeDtypeStruct((NT, D_MODEL), f32),
        grid=(NT // EMB_TM,),
        in_specs=[
            pl.BlockSpec((EMB_TM, D_MODEL), lambda i: (jnp.minimum(i, nc - 1), 0)),
            pl.BlockSpec((EMB_TM, D_MODEL), lambda i: (jnp.maximum(i - nc, 0), 0)),
            pl.BlockSpec((EMB_TM, D_MODEL), lambda i: (jnp.maximum(i - nc, 0) % per_seq, 0)),
        ],
        out_specs=pl.BlockSpec((EMB_TM, D_MODEL), lambda i: (i, 0)),
        compiler_params=_cparams(("arbitrary",)),
        name="embed",
    )(x_prompt.reshape(N_CTX, D_MODEL), x_sample.reshape(N_LAT, D_MODEL), pe)


def _grid_pos_embed():
    rows = DEC_SEQ // GRID_W
    rr, cc = jnp.meshgrid(jnp.arange(rows, dtype=f32), jnp.arange(GRID_W, dtype=f32), indexing="ij")
    quarter = D_MODEL // 4
    omega = 1.0 / (POS_BASE ** (jnp.arange(quarter, dtype=f32) / quarter))

    def emb(pos):
        ang = pos[:, None] * omega[None, :]
        return jnp.concatenate([jnp.sin(ang), jnp.cos(ang)], axis=-1)

    return jnp.concatenate([emb(rr.reshape(-1)), emb(cc.reshape(-1))], axis=-1)


def _pre_kernel(x_ref, mod_ref, g_ref, w_ref, wglr_ref, wg2_ref, bg_ref, proj_ref, logg_ref, h_ref):
    @pl.when(pl.program_id(1) == 0)
    def _():
        x = x_ref[...]
        y = x * lax.rsqrt(jnp.mean(x * x, axis=-1, keepdims=True) + EPS) * g_ref[...]
        h = y * (1.0 + mod_ref[0, 1:2, :]) + mod_ref[0, 0:1, :]
        hb = h.astype(bf16)
        h_ref[...] = hb
        glr = jnp.dot(hb, wglr_ref[...], preferred_element_type=f32)
        gate = jnp.dot(glr, wg2_ref[...], precision=HIGHEST, preferred_element_type=f32) + bg_ref[...]
        logg_ref[...] = _log_sigmoid(gate) * (1.0 / GLA_GATE_NORM)

    proj_ref[...] = jnp.dot(h_ref[...], w_ref[...], preferred_element_type=f32).astype(bf16)


def _pre(x, mod_l, norm1, w_main, w_glr, w_g2bd, b_g):
    return pl.pallas_call(
        _pre_kernel,
        out_shape=(jax.ShapeDtypeStruct((NT, PROJ_COLS), bf16),
                   jax.ShapeDtypeStruct((NT, N_DIRS * GLA_QK), f32)),
        grid=(NT // PRE_TM, PROJ_COLS // PRE_TN),
        in_specs=[
            pl.BlockSpec((PRE_TM, D_MODEL), lambda i, j: (i, 0)),
            pl.BlockSpec((1, 6, D_MODEL), lambda i, j: (_row_seq_of_tile(i, PRE_TM), 0, 0)),
            pl.BlockSpec((1, D_MODEL), lambda i, j: (0, 0)),
            pl.BlockSpec((D_MODEL, PRE_TN), lambda i, j: (0, j)),
            pl.BlockSpec((D_MODEL, N_DIRS * GLA_RANK), lambda i, j: (0, 0)),
            pl.BlockSpec((N_DIRS * GLA_RANK, N_DIRS * GLA_QK), lambda i, j: (0, 0)),
            pl.BlockSpec((1, N_DIRS * GLA_QK), lambda i, j: (0, 0)),
        ],
        out_specs=(pl.BlockSpec((PRE_TM, PRE_TN), lambda i, j: (i, j)),
                   pl.BlockSpec((PRE_TM, N_DIRS * GLA_QK), lambda i, j: (i, 0))),
        scratch_shapes=[pltpu.VMEM((PRE_TM, D_MODEL), bf16)],
        compiler_params=_cparams(("arbitrary", "arbitrary")),
        name="pre",
    )(x, mod_l, norm1, w_main, w_glr, w_g2bd, b_g)


def _s5_weights(lam_re, lam_im, log_dt, b_re, b_im, c_re, c_im, d):
    T, G, P, H = S5_T, S5_GROUPS, S5_STATE, S5_GROUP
    dt = jnp.exp(log_dt)[..., None]
    z_re, z_im = lam_re * dt, lam_im * dt
    mag = jnp.exp(z_re)
    a_re, a_im = mag * jnp.cos(z_im), mag * jnp.sin(z_im)
    den = lam_re * lam_re + lam_im * lam_im
    n_re, n_im = a_re - 1.0, a_im
    k_re = (n_re * lam_re + n_im * lam_im) / den
    k_im = (n_im * lam_re - n_re * lam_im) / den
    bb_re = k_re[..., None] * b_re - k_im[..., None] * b_im
    bb_im = k_re[..., None] * b_im + k_im[..., None] * b_re
    n = jnp.arange(T + 1, dtype=f32)[:, None, None, None]
    pmag = jnp.exp(n * z_re)
    pw_re, pw_im = pmag * jnp.cos(n * z_im), pmag * jnp.sin(n * z_im)
    abb_re = pw_re[..., None] * bb_re - pw_im[..., None] * bb_im
    abb_im = pw_re[..., None] * bb_im + pw_im[..., None] * bb_re
    kern = (jnp.einsum("dgop,ndgpi->ndgoi", c_re, abb_re[:T], precision=HIGHEST)
            - jnp.einsum("dgop,ndgpi->ndgoi", c_im, abb_im[:T], precision=HIGHEST))

    lag = np.arange(T)[None, :, None] - np.arange(T)[:, None, None]
    sel_f = jnp.asarray(lag == np.arange(T)[None, None, :], f32)
    sel_b = jnp.asarray(-lag == np.arange(T)[None, None, :], f32)
    kf = jnp.einsum("stn,ngoi->stgoi", sel_f, kern[:, 0], precision=HIGHEST)
    kb = jnp.einsum("stn,ngoi->stgoi", sel_b, kern[:, 1], precision=HIGHEST)
    eye_t = jnp.eye(T, dtype=f32)[:, :, None, None, None]
    eye_h = jnp.eye(H, dtype=f32)[None, None, None, :, :]
    dterm = eye_t * eye_h * d[None, None, :, :, None]
    toeplitz = jnp.transpose(kf + kb + dterm, (2, 0, 4, 1, 3)).reshape(G, T * H, T * H)

    def state_cols(powers, dirn):
        re = jnp.transpose(abb_re[powers, dirn], (1, 0, 3, 2)).reshape(G, T * H, P)
        im = jnp.transpose(abb_im[powers, dirn], (1, 0, 3, 2)).reshape(G, T * H, P)
        return [re, im, im, re]

    w_state = jnp.concatenate(state_cols(slice(T - 1, None, -1), 0) + state_cols(slice(0, T), 1), axis=-1)

    def carry_rows(powers, dirn):
        pr = pw_re[powers, dirn]
        pi = pw_im[powers, dirn]
        cr, ci = c_re[dirn], c_im[dirn]
        on_re = cr[None] * pr[:, :, None, :] - ci[None] * pi[:, :, None, :]
        on_im = -cr[None] * pi[:, :, None, :] - ci[None] * pr[:, :, None, :]
        to_cols = lambda m: jnp.transpose(m, (1, 3, 0, 2)).reshape(G, P, T * H)
        return [to_cols(on_re), to_cols(on_im)]

    w_carry = jnp.concatenate(carry_rows(slice(1, T + 1), 0) + carry_rows(slice(T, 0, -1), 1), axis=1)

    def coef_rows(dirn):
        ar, ai = pw_re[T, dirn], pw_im[T, dirn]
        return [jnp.concatenate([ar, ar], -1), jnp.concatenate([-ai, ai], -1), jnp.concatenate([ai, -ai], -1)]

    zero = jnp.zeros((G, 2 * P), f32)
    coef = jnp.stack(coef_rows(0) + coef_rows(1) + [zero, zero], axis=1)
    return toeplitz.astype(bf16), w_state.astype(bf16), w_carry.astype(bf16), coef


def _s5_kernel(u_ref, tz_ref, ws_ref, wc_ref, coef_ref, h0_ref, y_ref, fin_ref, s_ref, hin_ref):
    u = u_ref[0]
    s_ref[...] = jnp.dot(u, ws_ref[0], preferred_element_type=f32)
    co = coef_ref[0]
    a1f, a2f, a2sf = co[0:1], co[1:2], co[2:3]
    a1b, a2b, a2sb = co[3:4], co[4:5], co[5:6]
    P2 = 2 * S5_STATE

    def step(rf, rb, n_rows, carry):
        hf, hfs, hb, hbs = carry
        hin_ref[pl.ds(rf, n_rows), 0:P2] = hf
        hin_ref[pl.ds(rb, n_rows), P2:2 * P2] = hb
        sf = s_ref[pl.ds(rf, n_rows), 0:P2]
        sfs = s_ref[pl.ds(rf, n_rows), P2:2 * P2]
        sb = s_ref[pl.ds(rb, n_rows), 2 * P2:3 * P2]
        sbs = s_ref[pl.ds(rb, n_rows), 3 * P2:4 * P2]
        return (a1f * hf + a2f * hfs + sf, a1f * hfs + a2sf * hf + sfs,
                a1b * hb + a2b * hbs + sb, a1b * hbs + a2sb * hb + sbs)

    zero = jnp.zeros((BATCH, P2), f32)
    carry = (zero, zero, zero, zero)
    for c in range(S5_CTX_CHUNKS):
        carry = step(c * BATCH, (S5_CTX_CHUNKS - 1 - c) * BATCH, BATCH, carry)
    fin_ref[0] = jnp.concatenate([carry[0], carry[2]], axis=1)

    h0 = h0_ref[0]
    lat0 = S5_CTX_CHUNKS * BATCH

    carry = (h0[:, 0:P2], h0[:, P2:2 * P2], h0[:, 2 * P2:3 * P2], h0[:, 3 * P2:4 * P2])
    for c in range(S5_LAT_CHUNKS):
        carry = step(lat0 + c * DEC_BATCH, lat0 + (S5_LAT_CHUNKS - 1 - c) * DEC_BATCH, DEC_BATCH, carry)

    y = jnp.dot(u, tz_ref[0], preferred_element_type=f32)
    y = y + jnp.dot(hin_ref[...].astype(bf16), wc_ref[0], preferred_element_type=f32)
    y_ref[0] = y.astype(bf16)


def _s5_scan(u_g, toeplitz, w_state, w_carry, coef, h0):
    G, TH, P2 = S5_GROUPS, S5_T * S5_GROUP, 2 * S5_STATE
    return pl.pallas_call(
        _s5_kernel,
        out_shape=(jax.ShapeDtypeStruct((G, S5_ROWS, TH), bf16),
                   jax.ShapeDtypeStruct((G, BATCH, 2 * P2), f32)),
        grid=(G,),
        in_specs=[
            pl.BlockSpec((1, S5_ROWS, TH), lambda g: (g, 0, 0)),
            pl.BlockSpec((1, TH, TH), lambda g: (g, 0, 0)),
            pl.BlockSpec((1, TH, 4 * P2), lambda g: (g, 0, 0)),
            pl.BlockSpec((1, 2 * P2, TH), lambda g: (g, 0, 0)),
            pl.BlockSpec((1, 8, P2), lambda g: (g, 0, 0)),
            pl.BlockSpec((1, DEC_BATCH, 4 * P2), lambda g: (g, 0, 0)),
        ],
        out_specs=(pl.BlockSpec((1, S5_ROWS, TH), lambda g: (g, 0, 0)),
                   pl.BlockSpec((1, BATCH, 2 * P2), lambda g: (g, 0, 0))),
        scratch_shapes=[pltpu.VMEM((S5_ROWS, 4 * P2), f32), pltpu.VMEM((S5_ROWS, 2 * P2), f32)],
        compiler_params=_cparams(("arbitrary",)),
        name="s5_scan",
    )(u_g, toeplitz, w_state, w_carry, coef, h0)


def _s5_to_groups(u):
    G, H, T = S5_GROUPS, S5_GROUP, S5_T
    ctx = u[:N_CTX].reshape(BATCH, S5_CTX_CHUNKS, T, G, H)
    lat = u[N_CTX:].reshape(DEC_BATCH, S5_LAT_CHUNKS, T, G, H)
    ctx = jnp.transpose(ctx, (3, 1, 0, 2, 4)).reshape(G, S5_CTX_CHUNKS * BATCH, T * H)
    lat = jnp.transpose(lat, (3, 1, 0, 2, 4)).reshape(G, S5_LAT_CHUNKS * DEC_BATCH, T * H)
    return jnp.concatenate([ctx, lat], axis=1)


def _s5_from_groups(y):
    G, H, T = S5_GROUPS, S5_GROUP, S5_T
    nc = S5_CTX_CHUNKS * BATCH
    ctx = y[:, :nc].reshape(G, S5_CTX_CHUNKS, BATCH, T, H)
    lat = y[:, nc:].reshape(G, S5_LAT_CHUNKS, DEC_BATCH, T, H)
    ctx = jnp.transpose(ctx, (2, 1, 3, 0, 4)).reshape(N_CTX, G * H)
    lat = jnp.transpose(lat, (2, 1, 3, 0, 4)).reshape(N_LAT, G * H)
    return jnp.concatenate([ctx, lat], axis=0)


def _s5_init_rows(s_re, s_im):
    parts = [s_re[:, 0], s_im[:, 0], s_im[:, 0], s_re[:, 0], s_re[:, 1], s_im[:, 1], s_im[:, 1], s_re[:, 1]]
    return jnp.transpose(jnp.concatenate(parts, axis=-1), (1, 0, 2))


def _gla_seq_of_chunk(cj):
    per_ctx = SEQ // GLA_C
    per_lat = DEC_SEQ // GLA_C
    return jnp.where(cj < GLA_CTX_CHUNKS, cj // per_ctx, BATCH + (cj - GLA_CTX_CHUNKS) // per_lat)


def _gla_kernel(q_ref, k_ref, v_ref, lg_ref, s0_ref, o_ref, fin_ref, st_ref, a_ref, *, reverse):
    j = pl.program_id(0)
    cj = (GLA_CHUNKS - 1 - j) if reverse else j
    per_ctx = SEQ // GLA_C
    per_lat = DEC_SEQ // GLA_C
    pos = jnp.where(cj < GLA_CTX_CHUNKS, cj % per_ctx, (cj - GLA_CTX_CHUNKS) % per_lat)
    n_in_seq = jnp.where(cj < GLA_CTX_CHUNKS, per_ctx, per_lat)
    is_first = (pos == n_in_seq - 1) if reverse else (pos == 0)
    is_last = (pos == 0) if reverse else (pos == n_in_seq - 1)
    is_ctx = cj < GLA_CTX_CHUNKS

    @pl.when(is_first & is_ctx)
    def _():
        st_ref[...] = jnp.zeros(st_ref.shape, f32)

    @pl.when(is_first & jnp.logical_not(is_ctx))
    def _():
        st_ref[...] = s0_ref[0, 0, 0]

    C, SUB = GLA_C, GLA_SUB
    row = lax.broadcasted_iota(i32, (C, C), 0)
    col = lax.broadcasted_iota(i32, (C, C), 1)
    if reverse:
        m_cum = (col >= row)
        m_ref = (col >= (row // SUB + 1) * SUB)
    else:
        m_cum = (col <= row)
        m_ref = (col < (row // SUB) * SUB)
    m_both = jnp.concatenate([m_cum.astype(f32), m_ref.astype(f32)], axis=0)
    lg_all = lg_ref[...]
    sums = jnp.dot(m_both, lg_all, precision=HIGHEST, preferred_element_type=f32)
    scale = GLA_DK ** -0.5
    a_ref[...] = jnp.zeros((C, C), f32)

    for h in range(GLA_HEADS):
        ks = slice(h * GLA_DK, (h + 1) * GLA_DK)
        vs = slice(h * GLA_DV, (h + 1) * GLA_DV)
        lg = lg_all[:, ks]
        cum = sums[0:C, ks]
        ref = sums[C:2 * C, ks]
        tot = jnp.sum(lg, axis=0, keepdims=True)
        q = q_ref[:, ks].astype(f32) * scale
        k = k_ref[:, ks].astype(f32)
        v = v_ref[:, vs]
        state = st_ref[h]

        q_in = (q * jnp.exp(cum)).astype(bf16)
        o = jnp.dot(q_in, state.astype(bf16), preferred_element_type=f32)
        k_st = k * jnp.exp(tot - cum)
        tot_col = jnp.sum(lg.T, axis=1, keepdims=True)
        st_ref[h] = jnp.exp(tot_col) * state + jnp.dot(k_st.T.astype(bf16), v, preferred_element_type=f32)

        qt = (q * jnp.exp(cum - ref)).astype(bf16)
        for blk in range(GLA_NSUB):
            rows = slice(blk * SUB, (blk + 1) * SUB)
            cols = slice((blk + 1) * SUB, C) if reverse else slice(0, blk * SUB)
            if cols.start == cols.stop:
                continue
            kt = (k[cols] * jnp.exp(ref[blk * SUB:blk * SUB + 1] - cum[cols])).astype(bf16)
            a_ref[rows, cols] = lax.dot_general(qt[rows], kt, (((1,), (1,)), ((), ())), preferred_element_type=f32)

        diag = jnp.zeros((C, C), f32)
        for lag in range(SUB):
            shift = (C - lag) % C if reverse else lag
            k_sh = k if lag == 0 else pltpu.roll(k, shift, axis=0)
            c_sh = cum if lag == 0 else pltpu.roll(cum, shift, axis=0)
            z = q * k_sh * jnp.exp(jnp.minimum(cum - c_sh, 0.0))
            r = jnp.sum(z, axis=1, keepdims=True)
            if reverse:
                hit = (col == row + lag) & (row % SUB + lag < SUB)
            else:
                hit = (col == row - lag) & (row % SUB >= lag)
            diag = diag + jnp.where(hit, r, 0.0)
        scores = (a_ref[...] + diag).astype(bf16)
        o_ref[:, vs] = o + jnp.dot(scores, v, preferred_element_type=f32)

    @pl.when(is_last & is_ctx)
    def _():
        fin_ref[0] = st_ref[...]


def _gla_direction(proj, logg, state_gla, layer, reverse):
    dirn = 1 if reverse else 0
    seq = lambda j: _gla_seq_of_chunk(cidx(j))
    cidx = (lambda j: GLA_CHUNKS - 1 - j) if reverse else (lambda j: j)
    q_blk = (S5_WIDTH) // GLA_QK
    v_blk = (S5_WIDTH + 2 * GLA_QK) // GLA_V
    return pl.pallas_call(
        functools.partial(_gla_kernel, reverse=reverse),
        out_shape=(jax.ShapeDtypeStruct((NT, GLA_V), f32),
                   jax.ShapeDtypeStruct((BATCH, GLA_HEADS, GLA_DK, GLA_DV), f32)),
        grid=(GLA_CHUNKS,),
        in_specs=[
            pl.BlockSpec((GLA_C, GLA_QK), lambda j: (cidx(j), q_blk)),
            pl.BlockSpec((GLA_C, GLA_QK), lambda j: (cidx(j), q_blk + 1)),
            pl.BlockSpec((GLA_C, GLA_V), lambda j: (cidx(j), v_blk)),
            pl.BlockSpec((GLA_C, GLA_QK), lambda j: (cidx(j), dirn)),
            pl.BlockSpec((1, 1, 1, GLA_HEADS, GLA_DK, GLA_DV),
                         lambda j: (jnp.maximum(seq(j) - BATCH, 0), layer, dirn, 0, 0, 0)),
        ],
        out_specs=(pl.BlockSpec((GLA_C, GLA_V), lambda j: (cidx(j), 0)),
                   pl.BlockSpec((1, GLA_HEADS, GLA_DK, GLA_DV), lambda j: (jnp.minimum(seq(j), BATCH - 1), 0, 0, 0))),
        scratch_shapes=[pltpu.VMEM((GLA_HEADS, GLA_DK, GLA_DV), f32), pltpu.VMEM((GLA_C, GLA_C), f32)],
        compiler_params=_cparams(("arbitrary",)),
        name="gla_bwd" if reverse else "gla_fwd",
    )(proj, proj, proj, logg, state_gla)


def _post_kernel(y_ref, of_ref, ob_ref, r_ref, ga_ref, gb_ref, x_ref, mod_ref,
                 wglu_ref, bglu_ref, ws5_ref, gn_ref, wgla_ref, wout_ref, n2_ref, wr_ref, br_ref,
                 x1_ref, h2_ref, gate_ref, idx_ref):
    g = jax.nn.gelu(y_ref[...].astype(f32))
    glu = jnp.dot(g.astype(bf16), wglu_ref[...], preferred_element_type=f32) + bglu_ref[...]
    ya = g * jax.nn.sigmoid(glu)

    o = of_ref[...] + ob_ref[...]
    heads = []
    for h in range(GLA_HEADS):
        oh = o[:, h * GLA_DV:(h + 1) * GLA_DV]
        heads.append(oh * lax.rsqrt(jnp.mean(oh * oh, axis=-1, keepdims=True) + EPS))
    r = r_ref[...].astype(f32)
    yb = jnp.concatenate(heads, axis=1) * gn_ref[...] * (r * jax.nn.sigmoid(r))

    merged = (jax.nn.sigmoid(ga_ref[...].astype(f32)) * jnp.dot(ya.astype(bf16), ws5_ref[...], preferred_element_type=f32)
              + jax.nn.sigmoid(gb_ref[...].astype(f32)) * jnp.dot(yb.astype(bf16), wgla_ref[...], preferred_element_type=f32))
    x1 = x_ref[...] + mod_ref[0, 2:3, :] * jnp.dot(merged.astype(bf16), wout_ref[...], preferred_element_type=f32)
    x1_ref[...] = x1

    y2 = x1 * lax.rsqrt(jnp.mean(x1 * x1, axis=-1, keepdims=True) + EPS) * n2_ref[...]
    h2 = y2 * (1.0 + mod_ref[0, 4:5, :]) + mod_ref[0, 3:4, :]
    h2_ref[...] = h2

    logits = jnp.dot(h2, wr_ref[...], precision=HIGHEST, preferred_element_type=f32) + br_ref[...]
    lane = lax.broadcasted_iota(i32, logits.shape, 1)
    lane_f = lane.astype(f32)
    vals = logits
    top_v, top_i = [], []
    for _ in range(TOP_K):
        m = jnp.max(vals, axis=-1, keepdims=True)
        idx = jnp.min(jnp.where(vals == m, lane_f, float(LANES)), axis=-1, keepdims=True).astype(i32)
        top_v.append(m)
        top_i.append(idx)
        vals = jnp.where(lane == idx, -jnp.inf, vals)
    ex = [jnp.exp(v - top_v[0]) for v in top_v]
    inv = 1.0 / (ex[0] + ex[1] + ex[2] + ex[3])
    gates = jnp.zeros(logits.shape, f32)
    idxs = jnp.zeros(logits.shape, i32)
    for kk in range(TOP_K):
        gates = jnp.where(lane == kk, ex[kk] * inv, gates)
        idxs = jnp.where(lane == kk, top_i[kk], idxs)
    gate_ref[...] = gates
    idx_ref[...] = idxs


def _post(y_s5, o_f, o_b, proj, x, mod_l, wglu, bglu, ws5, gnorm, wgla, wout, norm2, wr, br):
    const = lambda shape: pl.BlockSpec(shape, lambda i: (0,) * len(shape), pipeline_mode=pl.Buffered(1))
    r_blk = (S5_WIDTH + 2 * GLA_QK + GLA_V) // GLA_V
    ga_blk = 4096 // D_MODEL
    return pl.pallas_call(
        _post_kernel,
        out_shape=(jax.ShapeDtypeStruct((NT, D_MODEL), f32),
                   jax.ShapeDtypeStruct((NT, D_MODEL), f32),
                   jax.ShapeDtypeStruct((NT, LANES), f32),
                   jax.ShapeDtypeStruct((NT, LANES), i32)),
        grid=(NT // POST_TM,),
        in_specs=[
            pl.BlockSpec((POST_TM, S5_WIDTH), lambda i: (i, 0)),
            pl.BlockSpec((POST_TM, GLA_V), lambda i: (i, 0)),
            pl.BlockSpec((POST_TM, GLA_V), lambda i: (i, 0)),
            pl.BlockSpec((POST_TM, GLA_V), lambda i: (i, r_blk)),
            pl.BlockSpec((POST_TM, D_MODEL), lambda i: (i, ga_blk)),
            pl.BlockSpec((POST_TM, D_MODEL), lambda i: (i, ga_blk + 1)),
            pl.BlockSpec((POST_TM, D_MODEL), lambda i: (i, 0)),
            pl.BlockSpec((1, 6, D_MODEL), lambda i: (_row_seq_of_tile(i, POST_TM), 0, 0)),
            const((S5_WIDTH, S5_WIDTH)), const((1, S5_WIDTH)), const((S5_WIDTH, D_MODEL)),
            const((1, GLA_V)), const((GLA_V, D_MODEL)), const((D_MODEL, D_MODEL)), const((1, D_MODEL)),
            const((D_MODEL, LANES)), const((1, LANES)),
        ],
        out_specs=(pl.BlockSpec((POST_TM, D_MODEL), lambda i: (i, 0)),
                   pl.BlockSpec((POST_TM, D_MODEL), lambda i: (i, 0)),
                   pl.BlockSpec((POST_TM, LANES), lambda i: (i, 0)),
                   pl.BlockSpec((POST_TM, LANES), lambda i: (i, 0))),
        compiler_params=_cparams(("arbitrary",)),
        name="post",
    )(y_s5, o_f, o_b, proj, proj, proj, x, mod_l, wglu, bglu, ws5, gnorm, wgla, wout, norm2, wr, br)


def _routing(top_idx):
    m = NT * TOP_K
    e_flat = top_idx.reshape(-1)
    experts = jnp.arange(N_EXPERTS, dtype=i32)
    onehot = (e_flat[:, None] == experts[None, :]).astype(i32)
    csum = jnp.cumsum(onehot, axis=0)
    counts = csum[-1]
    starts = jnp.cumsum(counts) - counts
    nblk = (counts + MOE_BM - 1) // MOE_BM
    blk_end = jnp.cumsum(nblk)
    blk_start = blk_end - nblk
    dest = jnp.sum(onehot * (blk_start[None, :] * MOE_BM + csum - 1), axis=1).astype(i32)

    order = jnp.argsort(e_flat, stable=True).astype(i32)
    p = jnp.arange(MOE_ROWS, dtype=i32)
    e_p = jnp.minimum(jnp.sum((blk_end[None, :] <= (p // MOE_BM)[:, None]).astype(i32), axis=1), N_EXPERTS - 1)
    oh_p = (e_p[:, None] == experts[None, :]).astype(i32)
    pick = lambda v: jnp.sum(oh_p * v[None, :], axis=1)
    idx_in = p - pick(blk_start) * MOE_BM
    valid = (p < blk_end[-1] * MOE_BM) & (idx_in < pick(counts))
    src = order[jnp.clip(pick(starts) + idx_in, 0, m - 1)] // TOP_K
    row_tok = jnp.where(valid, src, 0).astype(i32)

    n_units = (nblk + MOE_RB - 1) // MOE_RB
    unit_end = jnp.cumsum(n_units)
    unit_start = unit_end - n_units
    total_units = unit_end[-1]
    u = jnp.arange(MOE_UNITS, dtype=i32)
    used = u < total_units
    e_of_u = jnp.minimum(jnp.sum((unit_end[None, :] <= u[:, None]).astype(i32), axis=1), N_EXPERTS - 1)
    local = u - unit_start[e_of_u]
    u_blk0 = blk_start[e_of_u] + local * MOE_RB
    u_nblk = jnp.minimum(MOE_RB, nblk[e_of_u] - local * MOE_RB)
    last_e = e_of_u[jnp.maximum(total_units - 1, 0)]
    u_exp = jnp.where(used, e_of_u, last_e).astype(i32)
    u_blk0 = jnp.where(used, u_blk0, 0).astype(i32)
    u_nblk = jnp.where(used, u_nblk, 0).astype(i32)
    return row_tok, dest, u_exp, u_blk0, u_nblk, blk_end[-1:].astype(i32)


def _gather_kernel(tok_ref, src_ref, o_ref, stage_ref, sem):
    i = pl.program_id(0)
    S = GATHER_SPLIT

    def row_copy(t, slot, r):
        return pltpu.make_async_copy(src_ref.at[pl.ds(pl.multiple_of(t * S, S), S), :],
                                     stage_ref.at[slot, pl.ds(pl.multiple_of(r * S, S), S), :], sem.at[slot])

    def issue(step, slot):
        base = step * GATHER_GB
        lax.fori_loop(0, GATHER_GB, lambda r, c: (row_copy(tok_ref[base + r], slot, r).start(), c)[1], 0, unroll=4)

    @pl.when(i == 0)
    def _():
        issue(0, 0)

    @pl.when(i + 1 < pl.num_programs(0))
    def _():
        issue(i + 1, (i + 1) % 2)

    def finish(slot):
        lax.fori_loop(0, GATHER_GB, lambda r, c: (row_copy(0, slot, r).wait(), c)[1], 0, unroll=8)
        parts = [stage_ref[slot, pl.ds(k, GATHER_GB, stride=S), :] for k in range(S)]
        o_ref[...] = jnp.concatenate(parts, axis=1).astype(bf16)

    for slot in range(2):
        pl.when(i % 2 == slot)(functools.partial(finish, slot))


def _gather_rows(h2, row_tok):
    S = GATHER_SPLIT
    return pl.pallas_call(
        _gather_kernel,
        out_shape=jax.ShapeDtypeStruct((MOE_ROWS, D_MODEL), bf16),
        grid_spec=pltpu.PrefetchScalarGridSpec(
            num_scalar_prefetch=1,
            grid=(MOE_ROWS // GATHER_GB,),
            in_specs=[pl.BlockSpec(memory_space=pl.ANY)],
            out_specs=pl.BlockSpec((GATHER_GB, D_MODEL), lambda i, tok: (i, 0)),
            scratch_shapes=[pltpu.VMEM((2, GATHER_GB * S, D_MODEL // S), f32), pltpu.SemaphoreType.DMA((2,))],
        ),
        compiler_params=_cparams(("arbitrary",)),
        name="moe_gather",
    )(row_tok, h2.reshape(NT * S, D_MODEL // S))


def _deinterleave_matrix():
    half = MOE_TA // 2
    src = jnp.arange(MOE_TA)[:, None]
    dst = jnp.arange(MOE_TA)[None, :]
    return (dst == (src % 2) * half + src // 2).astype(bf16)


def _moe_kernel(uexp_ref, ublk_ref, unb_ref, nused_ref, x_hbm, w1_ref, b1_ref, perm_ref, w2_ref, b2_ref,
                y_hbm, x_buf, act_buf, w_buf, y_buf, zero_buf, sem_in, sem_out):
    u = pl.program_id(0)
    j = pl.program_id(1)
    nb = unb_ref[u]
    row0 = ublk_ref[u] * MOE_BM
    big = MOE_QB * MOE_BM
    n_big = nb // MOE_QB
    n_small = nb - n_big * MOE_QB
    small0 = n_big * big

    def in_copy(r):
        rows = pl.ds(pl.multiple_of(r * MOE_BM, MOE_BM), MOE_BM)
        src = x_hbm.at[pl.ds(pl.multiple_of(row0 + r * MOE_BM, MOE_BM), MOE_BM), :]
        return pltpu.make_async_copy(src, x_buf.at[rows, :], sem_in)

    @pl.when((j == 0) & (nb > 0))
    def _():
        lax.fori_loop(0, nb, lambda r, c: (in_copy(r).start(), c)[1], 0)
        lax.fori_loop(0, nb, lambda r, c: (in_copy(r).wait(), c)[1], 0)

    @pl.when((j < MOE_J1) & (nb > 0))
    def _():
        w_buf[...] = w1_ref[0, 0].astype(bf16)
        half = MOE_TA // 2

        def act_rows(start, m):
            rows = pl.ds(pl.multiple_of(start, MOE_BM), m)
            h = jnp.dot(x_buf[rows, :], w_buf[...], preferred_element_type=f32) + b1_ref[0, 0]
            hb = h.astype(bf16)
            parts = []
            for s in range(MOE_TN // MOE_TA):
                hp = jnp.dot(hb[:, s * MOE_TA:(s + 1) * MOE_TA], perm_ref[...], preferred_element_type=f32)
                hg = jnp.minimum(hp[:, :half], SWIGLU_LIMIT)
                hl = jnp.clip(hp[:, half:], -SWIGLU_LIMIT, SWIGLU_LIMIT)
                parts.append(hg * jax.nn.sigmoid(SWIGLU_ALPHA * hg) * (hl + 1.0))
            act_buf[j, rows, :] = jnp.concatenate(parts, axis=1).astype(bf16)

        lax.fori_loop(0, n_big, lambda q, c: (act_rows(q * big, big), c)[1], 0)
        lax.fori_loop(0, n_small, lambda r, c: (act_rows(small0 + r * MOE_BM, MOE_BM), c)[1], 0)

    @pl.when((j >= MOE_J1) & (nb > 0))
    def _():
        w_buf[...] = w2_ref[0, 0].astype(bf16)
        col0 = pl.multiple_of((j - MOE_J1) * MOE_TN, MOE_TN)

        def out_copy(start, m):
            rows = pl.ds(pl.multiple_of(start, MOE_BM), m)
            dst = y_hbm.at[pl.ds(pl.multiple_of(row0 + start, MOE_BM), m), pl.ds(col0, MOE_TN)]
            return pltpu.make_async_copy(y_buf.at[rows, :], dst, sem_out)

        def out_rows(start, m):
            rows = pl.ds(pl.multiple_of(start, MOE_BM), m)
            acc = b2_ref[0, 0] + jnp.zeros((m, MOE_TN), f32)
            for jj in range(MOE_J1):
                acc = acc + jnp.dot(act_buf[jj, rows, :], w_buf[jj * MOE_TA:(jj + 1) * MOE_TA, :],
                                    preferred_element_type=f32)
            y_buf[rows, :] = acc
            out_copy(start, m).start()

        lax.fori_loop(0, n_big, lambda q, c: (out_rows(q * big, big), c)[1], 0)
        lax.fori_loop(0, n_small, lambda r, c: (out_rows(small0 + r * MOE_BM, MOE_BM), c)[1], 0)
        lax.fori_loop(0, n_big, lambda q, c: (out_copy(q * big, big).wait(), c)[1], 0)
        lax.fori_loop(0, n_small, lambda r, c: (out_copy(small0 + r * MOE_BM, MOE_BM).wait(), c)[1], 0)

    @pl.when((u == MOE_UNITS - 1) & (j == MOE_J1 + MOE_J2 - 1))
    def _():
        zero_buf[...] = jnp.zeros(zero_buf.shape, f32)

        def fill_copy(r):
            dst = y_hbm.at[pl.ds(pl.multiple_of(r * MOE_BM, MOE_BM), MOE_BM), :]
            return pltpu.make_async_copy(zero_buf, dst, sem_out)

        lax.fori_loop(nused_ref[0], MOE_BLOCKS, lambda r, c: (fill_copy(r).start(), c)[1], 0)
        lax.fori_loop(nused_ref[0], MOE_BLOCKS, lambda r, c: (fill_copy(r).wait(), c)[1], 0)


def _moe_experts(x_sorted, u_exp, u_blk0, u_nblk, n_used, w1, b1, w2, b2, layer):
    def w1_idx(u, j, uexp, ublk, unb, nused):
        return (layer, uexp[u], 0, jnp.where(unb[u] > 0, jnp.minimum(j, MOE_J1 - 1), MOE_J1 - 1))

    def w2_idx(u, j, uexp, ublk, unb, nused):
        return (layer, uexp[u], 0, jnp.where(unb[u] > 0, jnp.maximum(j - MOE_J1, 0), MOE_J2 - 1))

    return pl.pallas_call(
        _moe_kernel,
        out_shape=jax.ShapeDtypeStruct((MOE_ROWS, D_MODEL), f32),
        grid_spec=pltpu.PrefetchScalarGridSpec(
            num_scalar_prefetch=4,
            grid=(MOE_UNITS, MOE_J1 + MOE_J2),
            in_specs=[
                pl.BlockSpec(memory_space=pl.ANY),
                pl.BlockSpec((1, 1, D_MODEL, MOE_TN), w1_idx),
                pl.BlockSpec((1, 1, 1, MOE_TN), w1_idx),
                pl.BlockSpec((MOE_TA, MOE_TA), lambda u, j, *_: (0, 0)),
                pl.BlockSpec((1, 1, D_MODEL, MOE_TN), w2_idx),
                pl.BlockSpec((1, 1, 1, MOE_TN), w2_idx),
            ],
            out_specs=pl.BlockSpec(memory_space=pl.ANY),
            scratch_shapes=[
                pltpu.VMEM((MOE_RMAX, D_MODEL), bf16),
                pltpu.VMEM((MOE_J1, MOE_RMAX, MOE_TA), bf16),
                pltpu.VMEM((D_MODEL, MOE_TN), bf16),
                pltpu.VMEM((MOE_RMAX, MOE_TN), f32),
                pltpu.VMEM((MOE_BM, D_MODEL), f32),
                pltpu.SemaphoreType.DMA(()),
                pltpu.SemaphoreType.DMA(()),
            ],
        ),
        compiler_params=_cparams(("arbitrary", "arbitrary")),
        name="moe_experts",
    )(u_exp, u_blk0, u_nblk, n_used, x_sorted, w1, b1, _deinterleave_matrix(), w2, b2)


def _combine_kernel(dest_ref, y_hbm, gate_ref, x1_ref, mod_ref, nf_ref, o_ref, y_buf, sem, *, final_norm):
    i = pl.program_id(0)

    def row_copy(p, slot, kk, r):
        return pltpu.make_async_copy(y_hbm.at[pl.ds(p, 1), :], y_buf.at[slot, kk, pl.ds(r, 1), :], sem.at[slot])

    def issue(step, slot):
        base = step * COMB_TM * TOP_K

        def body(r, c):
            for kk in range(TOP_K):
                row_copy(dest_ref[base + r * TOP_K + kk], slot, kk, r).start()
            return c

        lax.fori_loop(0, COMB_TM, body, 0, unroll=2)

    @pl.when(i == 0)
    def _():
        issue(0, 0)

    @pl.when(i + 1 < pl.num_programs(0))
    def _():
        issue(i + 1, (i + 1) % 2)

    def finish(slot):
        def wait(r, c):
            for kk in range(TOP_K):
                row_copy(0, slot, kk, r).wait()
            return c

        lax.fori_loop(0, COMB_TM, wait, 0, unroll=4)
        gates = gate_ref[...]
        acc = jnp.zeros((COMB_TM, D_MODEL), f32)
        for kk in range(TOP_K):
            acc = acc + gates[:, kk:kk + 1] * y_buf[slot, kk]
        x2 = x1_ref[...] + mod_ref[0, 5:6, :] * acc
        if final_norm:
            x2 = x2 * lax.rsqrt(jnp.mean(x2 * x2, axis=-1, keepdims=True) + EPS) * nf_ref[...]
        o_ref[...] = x2

    for slot in range(2):
        pl.when(i % 2 == slot)(functools.partial(finish, slot))


def _combine(y_sorted, dest, gates, x1, mod_l, norm_f, final_norm):
    return pl.pallas_call(
        functools.partial(_combine_kernel, final_norm=final_norm),
        out_shape=jax.ShapeDtypeStruct((NT, D_MODEL), f32),
        grid_spec=pltpu.PrefetchScalarGridSpec(
            num_scalar_prefetch=1,
            grid=(NT // COMB_TM,),
            in_specs=[
                pl.BlockSpec(memory_space=pl.ANY),
                pl.BlockSpec((COMB_TM, LANES), lambda i, d: (i, 0)),
                pl.BlockSpec((COMB_TM, D_MODEL), lambda i, d: (i, 0)),
                pl.BlockSpec((1, 6, D_MODEL), lambda i, d: (_row_seq_of_tile(i, COMB_TM), 0, 0)),
                pl.BlockSpec((1, D_MODEL), lambda i, d: (0, 0)),
            ],
            out_specs=pl.BlockSpec((COMB_TM, D_MODEL), lambda i, d: (i, 0)),
            scratch_shapes=[pltpu.VMEM((2, TOP_K, COMB_TM, D_MODEL), f32), pltpu.SemaphoreType.DMA((2,))],
        ),
        compiler_params=_cparams(("arbitrary",)),
        name="moe_combine_final" if final_norm else "moe_combine",
    )(dest, y_sorted, gates, x1, mod_l, norm_f)


def kernel(x_prompt, x_sample, c, state_s5_re, state_s5_im, state_gla, c_ctx, w_mod, b_mod, norm1, w_in, s5_lam_re, s5_lam_im, s5_log_dt, s5_b_re, s5_b_im, s5_c_re, s5_c_im, s5_d, s5_w_glu, s5_b_glu, w_s5_out, gla_w_g2, gla_b_g, gla_norm, w_gla_out, w_out, norm2, w_router, b_router, w_e1, b_e1, w_e2, b_e2, norm_f):
    cond = jnp.concatenate([c_ctx[None], c, jnp.zeros((N_MODROWS - 1 - DEC_BATCH, D_MODEL), f32)], axis=0)
    mod = _modulation(cond, w_mod, b_mod).reshape(DEPTH, N_MODROWS, 6, D_MODEL)
    x = _embed(x_prompt, x_sample, _grid_pos_embed())

    glr0 = S5_WIDTH + 2 * GLA_QK + 2 * GLA_V
    glr1 = glr0 + N_DIRS * GLA_RANK
    s5_re_out, s5_im_out, gla_out = [], [], []
    toeplitz, w_state, w_carry, coef = jax.vmap(_s5_weights)(
        s5_lam_re, s5_lam_im, s5_log_dt, s5_b_re, s5_b_im, s5_c_re, s5_c_im, s5_d)
    b_e1_r = b_e1.reshape(DEPTH, N_EXPERTS, 1, 2 * D_MODEL)
    b_e2_r = b_e2.reshape(DEPTH, N_EXPERTS, 1, D_MODEL)
    for l in range(DEPTH):
        mod_l = mod[l]
        w_main = jnp.concatenate([w_in[l, :, :glr0], w_in[l, :, glr1:]], axis=1).astype(bf16)
        w_glr = w_in[l, :, glr0:glr1].astype(bf16)
        zg = jnp.zeros((GLA_RANK, GLA_QK), f32)
        w_g2bd = jnp.concatenate([jnp.concatenate([gla_w_g2[l, 0], zg], axis=1),
                                  jnp.concatenate([zg, gla_w_g2[l, 1]], axis=1)], axis=0)
        b_g = gla_b_g[l].reshape(1, N_DIRS * GLA_QK)
        h0 = _s5_init_rows(state_s5_re[:, l], state_s5_im[:, l])
        w_r = jnp.concatenate([w_router[l], jnp.zeros((D_MODEL, LANES - N_EXPERTS), f32)], axis=1)
        b_r = jnp.concatenate([b_router[l], jnp.full((LANES - N_EXPERTS,), -jnp.inf, f32)]).reshape(1, LANES)

        proj, logg = _pre(x, mod_l, norm1[l].reshape(1, D_MODEL), w_main, w_glr, w_g2bd, b_g)
        y_g, s5_fin = _s5_scan(_s5_to_groups(proj[:, :S5_WIDTH]), toeplitz[l], w_state[l], w_carry[l], coef[l], h0)
        y_s5 = _s5_from_groups(y_g)
        o_f, st_f = _gla_direction(proj, logg, state_gla, l, False)
        o_b, st_b = _gla_direction(proj, logg, state_gla, l, True)

        x1, h2, gates, top_idx = _post(
            y_s5, o_f, o_b, proj, x, mod_l,
            s5_w_glu[l].astype(bf16), s5_b_glu[l].reshape(1, S5_WIDTH), w_s5_out[l].astype(bf16),
            jnp.tile(gla_norm[l], GLA_HEADS).reshape(1, GLA_V), w_gla_out[l].astype(bf16), w_out[l].astype(bf16),
            norm2[l].reshape(1, D_MODEL), w_r, b_r)

        row_tok, dest, u_exp, u_blk0, u_nblk, n_used = _routing(top_idx[:, :TOP_K])
        x_sorted = _gather_rows(h2, row_tok)
        y_sorted = _moe_experts(x_sorted, u_exp, u_blk0, u_nblk, n_used, w_e1, b_e1_r, w_e2, b_e2_r, l)
        x = _combine(y_sorted, dest, gates, x1, mod_l, norm_f.reshape(1, D_MODEL), l == DEPTH - 1)

        P = S5_STATE
        fin = jnp.transpose(s5_fin, (1, 0, 2))
        s5_re_out.append(jnp.stack([fin[..., 0:P], fin[..., 2 * P:3 * P]], axis=1))
        s5_im_out.append(jnp.stack([fin[..., P:2 * P], fin[..., 3 * P:4 * P]], axis=1))
        gla_out.append(jnp.stack([st_f, st_b], axis=1))

    y_prompt = x[:N_CTX].reshape(BATCH, SEQ, D_MODEL)
    y_sample = x[N_CTX:].reshape(DEC_BATCH, DEC_SEQ, D_MODEL)
    return (y_prompt, y_sample, jnp.stack(s5_re_out, axis=1), jnp.stack(s5_im_out, axis=1),
            jnp.stack(gla_out, axis=1))
```

```python
import functools
import math

import jax
import jax.numpy as jnp
import numpy as np
from jax import lax
from jax.experimental import pallas as pl
from jax.experimental.pallas import tpu as pltpu

f32 = jnp.float32
bf16 = jnp.bfloat16
i32 = jnp.int32
HIGHEST = lax.Precision.HIGHEST

D_MODEL = 2048
BATCH = 16
SEQ = 256
DEPTH = 2
DEC_BATCH = 2
DEC_SEQ = 2048
GRID_W = 64
N_DIRS = 2
S5_WIDTH = 1024
S5_GROUP = 16
S5_GROUPS = 64
S5_STATE = 64
GLA_HEADS = 4
GLA_DK = 128
GLA_DV = 256
GLA_QK = 512
GLA_V = 1024
GLA_RANK = 16
GLA_GATE_NORM = 16.0
N_EXPERTS = 32
TOP_K = 4
SWIGLU_LIMIT = 7.0
SWIGLU_ALPHA = 1.702
POS_BASE = 10000.0
EPS = 1e-6

N_CTX = BATCH * SEQ
N_LAT = DEC_BATCH * DEC_SEQ
NT = N_CTX + N_LAT
N_SEQS = BATCH + DEC_BATCH
N_MODROWS = 8

LANES = 128
VMEM_LIMIT = 56 * 1024 * 1024

S5_T = 16
S5_ROWS = NT // S5_T
S5_CTX_CHUNKS = SEQ // S5_T
S5_LAT_CHUNKS = DEC_SEQ // S5_T
GLA_C = 128
GLA_SUB = 16
GLA_NSUB = GLA_C // GLA_SUB
GLA_CHUNKS = NT // GLA_C
GLA_CTX_CHUNKS = N_CTX // GLA_C
PRE_TM = 1024
PRE_TN = 1024
PROJ_COLS = 8192
POST_TM = 256
MOE_BM = 128
MOE_ROWS = NT * TOP_K + N_EXPERTS * MOE_BM
MOE_BLOCKS = MOE_ROWS // MOE_BM
MOE_RB = 16
MOE_QB = 4
MOE_RMAX = MOE_RB * MOE_BM
MOE_UNITS = N_EXPERTS + MOE_BLOCKS // MOE_RB
MOE_TN = 512
MOE_TA = MOE_TN // 2
MOE_J1 = 2 * D_MODEL // MOE_TN
MOE_J2 = D_MODEL // MOE_TN
GATHER_GB = 512
GATHER_SPLIT = D_MODEL // LANES
COMB_TM = 128


def _cparams(sem, **kw):
    return pltpu.CompilerParams(dimension_semantics=sem, vmem_limit_bytes=VMEM_LIMIT, **kw)


def _row_seq_of_tile(i, tile_rows):
    first = i * tile_rows
    return jnp.where(first < N_CTX, 0, 1 + (first - N_CTX) // DEC_SEQ)


def _log_sigmoid(x):
    return -(jnp.maximum(-x, 0.0) + jnp.log1p(jnp.exp(-jnp.abs(x))))


MOD_TN = 1024


def _mod_kernel(c_ref, w_ref, b_ref, o_ref):
    c = c_ref[...]
    s = (c * jax.nn.sigmoid(c)).astype(bf16)
    o_ref[0] = jnp.dot(s, w_ref[0].astype(bf16), preferred_element_type=f32) + b_ref[0]


def _modulation(cond, w_mod, b_mod):
    n_out = 6 * D_MODEL
    return pl.pallas_call(
        _mod_kernel,
        out_shape=jax.ShapeDtypeStruct((DEPTH, N_MODROWS, n_out), f32),
        grid=(DEPTH, n_out // MOD_TN),
        in_specs=[
            pl.BlockSpec((N_MODROWS, D_MODEL), lambda l, j: (0, 0)),
            pl.BlockSpec((1, D_MODEL, MOD_TN), lambda l, j: (l, 0, j)),
            pl.BlockSpec((1, 1, MOD_TN), lambda l, j: (l, 0, j)),
        ],
        out_specs=pl.BlockSpec((1, N_MODROWS, MOD_TN), lambda l, j: (l, 0, j)),
        compiler_params=_cparams(("arbitrary", "arbitrary")),
        name="modulation",
    )(cond, w_mod, b_mod.reshape(DEPTH, 1, n_out))


EMB_TM = 512


def _embed_kernel(xp_ref, xs_ref, pe_ref, o_ref):
    i = pl.program_id(0)

    @pl.when(i < N_CTX // EMB_TM)
    def _():
        o_ref[...] = xp_ref[...]

    @pl.when(i >= N_CTX // EMB_TM)
    def _():
        o_ref[...] = xs_ref[...] + pe_ref[...]


def _embed(x_prompt, x_sample, pe):
    nc = N_CTX // EMB_TM
    per_seq = DEC_SEQ // EMB_TM
    return pl.pallas_call(
        _embed_kernel,
        out_shape=jax.ShapeDtypeStruct((NT, D_MODEL), f32),
        grid=(NT // EMB_TM,),
        in_specs=[
            pl.BlockSpec((EMB_TM, D_MODEL), lambda i: (jnp.minimum(i, nc - 1), 0)),
            pl.BlockSpec((EMB_TM, D_MODEL), lambda i: (jnp.maximum(i - nc, 0), 0)),
            pl.BlockSpec((EMB_TM, D_MODEL), lambda i: (jnp.maximum(i - nc, 0) % per_seq, 0)),
        ],
        out_specs=pl.BlockSpec((EMB_TM, D_MODEL), lambda i: (i, 0)),
        compiler_params=_cparams(("arbitrary",)),
        name="embed",
    )(x_prompt.reshape(N_CTX, D_MODEL), x_sample.reshape(N_LAT, D_MODEL), pe)


def _grid_pos_embed():
    rows = DEC_SEQ // GRID_W
    rr, cc = jnp.meshgrid(jnp.arange(rows, dtype=f32), jnp.arange(GRID_W, dtype=f32), indexing="ij")
    quarter = D_MODEL // 4
    omega = 1.0 / (POS_BASE ** (jnp.arange(quarter, dtype=f32) / quarter))

    def emb(pos):
        ang = pos[:, None] * omega[None, :]
        return jnp.concatenate([jnp.sin(ang), jnp.cos(ang)], axis=-1)

    return jnp.concatenate([emb(rr.reshape(-1)), emb(cc.reshape(-1))], axis=-1)


def _pre_kernel(x_ref, mod_ref, g_ref, w_ref, wglr_ref, wg2_ref, bg_ref, proj_ref, logg_ref, h_ref):
    @pl.when(pl.program_id(1) == 0)
    def _():
        x = x_ref[...]
        y = x * lax.rsqrt(jnp.mean(x * x, axis=-1, keepdims=True) + EPS) * g_ref[...]
        h = y * (1.0 + mod_ref[0, 1:2, :]) + mod_ref[0, 0:1, :]
        hb = h.astype(bf16)
        h_ref[...] = hb
        glr = jnp.dot(hb, wglr_ref[...], preferred_element_type=f32)
        gate = jnp.dot(glr, wg2_ref[...], precision=HIGHEST, preferred_element_type=f32) + bg_ref[...]
        logg_ref[...] = _log_sigmoid(gate) * (1.0 / GLA_GATE_NORM)

    proj_ref[...] = jnp.dot(h_ref[...], w_ref[...], preferred_element_type=f32).astype(bf16)


def _pre(x, mod_l, norm1, w_main, w_glr, w_g2bd, b_g):
    return pl.pallas_call(
        _pre_kernel,
        out_shape=(jax.ShapeDtypeStruct((NT, PROJ_COLS), bf16),
                   jax.ShapeDtypeStruct((NT, N_DIRS * GLA_QK), f32)),
        grid=(NT // PRE_TM, PROJ_COLS // PRE_TN),
        in_specs=[
            pl.BlockSpec((PRE_TM, D_MODEL), lambda i, j: (i, 0)),
            pl.BlockSpec((1, 6, D_MODEL), lambda i, j: (_row_seq_of_tile(i, PRE_TM), 0, 0)),
            pl.BlockSpec((1, D_MODEL), lambda i, j: (0, 0)),
            pl.BlockSpec((D_MODEL, PRE_TN), lambda i, j: (0, j)),
            pl.BlockSpec((D_MODEL, N_DIRS * GLA_RANK), lambda i, j: (0, 0)),
            pl.BlockSpec((N_DIRS * GLA_RANK, N_DIRS * GLA_QK), lambda i, j: (0, 0)),
            pl.BlockSpec((1, N_DIRS * GLA_QK), lambda i, j: (0, 0)),
        ],
        out_specs=(pl.BlockSpec((PRE_TM, PRE_TN), lambda i, j: (i, j)),
                   pl.BlockSpec((PRE_TM, N_DIRS * GLA_QK), lambda i, j: (i, 0))),
        scratch_shapes=[pltpu.VMEM((PRE_TM, D_MODEL), bf16)],
        compiler_params=_cparams(("arbitrary", "arbitrary")),
        name="pre",
    )(x, mod_l, norm1, w_main, w_glr, w_g2bd, b_g)


def _s5_weights(lam_re, lam_im, log_dt, b_re, b_im, c_re, c_im, d):
    T, G, P, H = S5_T, S5_GROUPS, S5_STATE, S5_GROUP
    dt = jnp.exp(log_dt)[..., None]
    z_re, z_im = lam_re * dt, lam_im * dt
    mag = jnp.exp(z_re)
    a_re, a_im = mag * jnp.cos(z_im), mag * jnp.sin(z_im)
    den = lam_re * lam_re + lam_im * lam_im
    n_re, n_im = a_re - 1.0, a_im
    k_re = (n_re * lam_re + n_im * lam_im) / den
    k_im = (n_im * lam_re - n_re * lam_im) / den
    bb_re = k_re[..., None] * b_re - k_im[..., None] * b_im
    bb_im = k_re[..., None] * b_im + k_im[..., None] * b_re
    n = jnp.arange(T + 1, dtype=f32)[:, None, None, None]
    pmag = jnp.exp(n * z_re)
    pw_re, pw_im = pmag * jnp.cos(n * z_im), pmag * jnp.sin(n * z_im)
    abb_re = pw_re[..., None] * bb_re - pw_im[..., None] * bb_im
    abb_im = pw_re[..., None] * bb_im + pw_im[..., None] * bb_re
    kern = (jnp.einsum("dgop,ndgpi->ndgoi", c_re, abb_re[:T], precision=HIGHEST)
            - jnp.einsum("dgop,ndgpi->ndgoi", c_im, abb_im[:T], precision=HIGHEST))

    lag = np.arange(T)[None, :, None] - np.arange(T)[:, None, None]
    sel_f = jnp.asarray(lag == np.arange(T)[None, None, :], f32)
    sel_b = jnp.asarray(-lag == np.arange(T)[None, None, :], f32)
    kf = jnp.einsum("stn,ngoi->stgoi", sel_f, kern[:, 0], precision=HIGHEST)
    kb = jnp.einsum("stn,ngoi->stgoi", sel_b, kern[:, 1], precision=HIGHEST)
    eye_t = jnp.eye(T, dtype=f32)[:, :, None, None, None]
    eye_h = jnp.eye(H, dtype=f32)[None, None, None, :, :]
    dterm = eye_t * eye_h * d[None, None, :, :, None]
    toeplitz = jnp.transpose(kf + kb + dterm, (2, 0, 4, 1, 3)).reshape(G, T * H, T * H)

    def state_cols(powers, dirn):
        re = jnp.transpose(abb_re[powers, dirn], (1, 0, 3, 2)).reshape(G, T * H, P)
        im = jnp.transpose(abb_im[powers, dirn], (1, 0, 3, 2)).reshape(G, T * H, P)
        return [re, im, im, re]

    w_state = jnp.concatenate(state_cols(slice(T - 1, None, -1), 0) + state_cols(slice(0, T), 1), axis=-1)

    def carry_rows(powers, dirn):
        pr = pw_re[powers, dirn]
        pi = pw_im[powers, dirn]
        cr, ci = c_re[dirn], c_im[dirn]
        on_re = cr[None] * pr[:, :, None, :] - ci[None] * pi[:, :, None, :]
        on_im = -cr[None] * pi[:, :, None, :] - ci[None] * pr[:, :, None, :]
        to_cols = lambda m: jnp.transpose(m, (1, 3, 0, 2)).reshape(G, P, T * H)
        return [to_cols(on_re), to_cols(on_im)]

    w_carry = jnp.concatenate(carry_rows(slice(1, T + 1), 0) + carry_rows(slice(T, 0, -1), 1), axis=1)

    def coef_rows(dirn):
        ar, ai = pw_re[T, dirn], pw_im[T, dirn]
        return [jnp.concatenate([ar, ar], -1), jnp.concatenate([-ai, ai], -1), jnp.concatenate([ai, -ai], -1)]

    zero = jnp.zeros((G, 2 * P), f32)
    coef = jnp.stack(coef_rows(0) + coef_rows(1) + [zero, zero], axis=1)
    return toeplitz.astype(bf16), w_state.astype(bf16), w_carry.astype(bf16), coef


def _s5_kernel(u_ref, tz_ref, ws_ref, wc_ref, coef_ref, h0_ref, y_ref, fin_ref, s_ref, hin_ref):
    u = u_ref[0]
    s_ref[...] = jnp.dot(u, ws_ref[0], preferred_element_type=f32)
    co = coef_ref[0]
    a1f, a2f, a2sf = co[0:1], co[1:2], co[2:3]
    a1b, a2b, a2sb = co[3:4], co[4:5], co[5:6]
    P2 = 2 * S5_STATE

    def step(rf, rb, n_rows, carry):
        hf, hfs, hb, hbs = carry
        hin_ref[pl.ds(rf, n_rows), 0:P2] = hf
        hin_ref[pl.ds(rb, n_rows), P2:2 * P2] = hb
        sf = s_ref[pl.ds(rf, n_rows), 0:P2]
        sfs = s_ref[pl.ds(rf, n_rows), P2:2 * P2]
        sb = s_ref[pl.ds(rb, n_rows), 2 * P2:3 * P2]
        sbs = s_ref[pl.ds(rb, n_rows), 3 * P2:4 * P2]
        return (a1f * hf + a2f * hfs + sf, a1f * hfs + a2sf * hf + sfs,
                a1b * hb + a2b * hbs + sb, a1b * hbs + a2sb * hb + sbs)

    zero = jnp.zeros((BATCH, P2), f32)
    carry = (zero, zero, zero, zero)
    for c in range(S5_CTX_CHUNKS):
        carry = step(c * BATCH, (S5_CTX_CHUNKS - 1 - c) * BATCH, BATCH, carry)
    fin_ref[0] = jnp.concatenate([carry[0], carry[2]], axis=1)

    h0 = h0_ref[0]
    lat0 = S5_CTX_CHUNKS * BATCH

    carry = (h0[:, 0:P2], h0[:, P2:2 * P2], h0[:, 2 * P2:3 * P2], h0[:, 3 * P2:4 * P2])
    for c in range(S5_LAT_CHUNKS):
        carry = step(lat0 + c * DEC_BATCH, lat0 + (S5_LAT_CHUNKS - 1 - c) * DEC_BATCH, DEC_BATCH, carry)

    y = jnp.dot(u, tz_ref[0], preferred_element_type=f32)
    y = y + jnp.dot(hin_ref[...].astype(bf16), wc_ref[0], preferred_element_type=f32)
    y_ref[0] = y.astype(bf16)


def _s5_scan(u_g, toeplitz, w_state, w_carry, coef, h0):
    G, TH, P2 = S5_GROUPS, S5_T * S5_GROUP, 2 * S5_STATE
    return pl.pallas_call(
        _s5_kernel,
        out_shape=(jax.ShapeDtypeStruct((G, S5_ROWS, TH), bf16),
                   jax.ShapeDtypeStruct((G, BATCH, 2 * P2), f32)),
        grid=(G,),
        in_specs=[
            pl.BlockSpec((1, S5_ROWS, TH), lambda g: (g, 0, 0)),
            pl.BlockSpec((1, TH, TH), lambda g: (g, 0, 0)),
            pl.BlockSpec((1, TH, 4 * P2), lambda g: (g, 0, 0)),
            pl.BlockSpec((1, 2 * P2, TH), lambda g: (g, 0, 0)),
            pl.BlockSpec((1, 8, P2), lambda g: (g, 0, 0)),
            pl.BlockSpec((1, DEC_BATCH, 4 * P2), lambda g: (g, 0, 0)),
        ],
        out_specs=(pl.BlockSpec((1, S5_ROWS, TH), lambda g: (g, 0, 0)),
                   pl.BlockSpec((1, BATCH, 2 * P2), lambda g: (g, 0, 0))),
        scratch_shapes=[pltpu.VMEM((S5_ROWS, 4 * P2), f32), pltpu.VMEM((S5_ROWS, 2 * P2), f32)],
        compiler_params=_cparams(("arbitrary",)),
        name="s5_scan",
    )(u_g, toeplitz, w_state, w_carry, coef, h0)


def _s5_to_groups(u):
    G, H, T = S5_GROUPS, S5_GROUP, S5_T
    ctx = u[:N_CTX].reshape(BATCH, S5_CTX_CHUNKS, T, G, H)
    lat = u[N_CTX:].reshape(DEC_BATCH, S5_LAT_CHUNKS, T, G, H)
    ctx = jnp.transpose(ctx, (3, 1, 0, 2, 4)).reshape(G, S5_CTX_CHUNKS * BATCH, T * H)
    lat = jnp.transpose(lat, (3, 1, 0, 2, 4)).reshape(G, S5_LAT_CHUNKS * DEC_BATCH, T * H)
    return jnp.concatenate([ctx, lat], axis=1)


def _s5_from_groups(y):
    G, H, T = S5_GROUPS, S5_GROUP, S5_T
    nc = S5_CTX_CHUNKS * BATCH
    ctx = y[:, :nc].reshape(G, S5_CTX_CHUNKS, BATCH, T, H)
    lat = y[:, nc:].reshape(G, S5_LAT_CHUNKS, DEC_BATCH, T, H)
    ctx = jnp.transpose(ctx, (2, 1, 3, 0, 4)).reshape(N_CTX, G * H)
    lat = jnp.transpose(lat, (2, 1, 3, 0, 4)).reshape(N_LAT, G * H)
    return jnp.concatenate([ctx, lat], axis=0)


def _s5_init_rows(s_re, s_im):
    parts = [s_re[:, 0], s_im[:, 0], s_im[:, 0], s_re[:, 0], s_re[:, 1], s_im[:, 1], s_im[:, 1], s_re[:, 1]]
    return jnp.transpose(jnp.concatenate(parts, axis=-1), (1, 0, 2))


def _gla_seq_of_chunk(cj):
    per_ctx = SEQ // GLA_C
    per_lat = DEC_SEQ // GLA_C
    return jnp.where(cj < GLA_CTX_CHUNKS, cj // per_ctx, BATCH + (cj - GLA_CTX_CHUNKS) // per_lat)


def _gla_kernel(q_ref, k_ref, v_ref, lg_ref, s0_ref, o_ref, fin_ref, st_ref, a_ref, *, reverse):
    j = pl.program_id(0)
    cj = (GLA_CHUNKS - 1 - j) if reverse else j
    per_ctx = SEQ // GLA_C
    per_lat = DEC_SEQ // GLA_C
    pos = jnp.where(cj < GLA_CTX_CHUNKS, cj % per_ctx, (cj - GLA_CTX_CHUNKS) % per_lat)
    n_in_seq = jnp.where(cj < GLA_CTX_CHUNKS, per_ctx, per_lat)
    is_first = (pos == n_in_seq - 1) if reverse else (pos == 0)
    is_last = (pos == 0) if reverse else (pos == n_in_seq - 1)
    is_ctx = cj < GLA_CTX_CHUNKS

    @pl.when(is_first & is_ctx)
    def _():
        st_ref[...] = jnp.zeros(st_ref.shape, f32)

    @pl.when(is_first & jnp.logical_not(is_ctx))
    def _():
        st_ref[...] = s0_ref[0, 0, 0]

    C, SUB = GLA_C, GLA_SUB
    row = lax.broadcasted_iota(i32, (C, C), 0)
    col = lax.broadcasted_iota(i32, (C, C), 1)
    if reverse:
        m_cum = (col >= row)
        m_ref = (col >= (row // SUB + 1) * SUB)
    else:
        m_cum = (col <= row)
        m_ref = (col < (row // SUB) * SUB)
    m_both = jnp.concatenate([m_cum.astype(f32), m_ref.astype(f32)], axis=0)
    lg_all = lg_ref[...]
    sums = jnp.dot(m_both, lg_all, precision=HIGHEST, preferred_element_type=f32)
    scale = GLA_DK ** -0.5
    a_ref[...] = jnp.zeros((C, C), f32)

    for h in range(GLA_HEADS):
        ks = slice(h * GLA_DK, (h + 1) * GLA_DK)
        vs = slice(h * GLA_DV, (h + 1) * GLA_DV)
        lg = lg_all[:, ks]
        cum = sums[0:C, ks]
        ref = sums[C:2 * C, ks]
        tot = jnp.sum(lg, axis=0, keepdims=True)
        q = q_ref[:, ks].astype(f32) * scale
        k = k_ref[:, ks].astype(f32)
        v = v_ref[:, vs]
        state = st_ref[h]

        q_in = (q * jnp.exp(cum)).astype(bf16)
        o = jnp.dot(q_in, state.astype(bf16), preferred_element_type=f32)
        k_st = k * jnp.exp(tot - cum)
        tot_col = jnp.sum(lg.T, axis=1, keepdims=True)
        st_ref[h] = jnp.exp(tot_col) * state + jnp.dot(k_st.T.astype(bf16), v, preferred_element_type=f32)

        qt = (q * jnp.exp(cum - ref)).astype(bf16)
        for blk in range(GLA_NSUB):
            rows = slice(blk * SUB, (blk + 1) * SUB)
            cols = slice((blk + 1) * SUB, C) if reverse else slice(0, blk * SUB)
            if cols.start == cols.stop:
                continue
            kt = (k[cols] * jnp.exp(ref[blk * SUB:blk * SUB + 1] - cum[cols])).astype(bf16)
            a_ref[rows, cols] = lax.dot_general(qt[rows], kt, (((1,), (1,)), ((), ())), preferred_element_type=f32)

        diag = jnp.zeros((C, C), f32)
        for lag in range(SUB):
            shift = (C - lag) % C if reverse else lag
            k_sh = k if lag == 0 else pltpu.roll(k, shift, axis=0)
            c_sh = cum if lag == 0 else pltpu.roll(cum, shift, axis=0)
            z = q * k_sh * jnp.exp(jnp.minimum(cum - c_sh, 0.0))
            r = jnp.sum(z, axis=1, keepdims=True)
            if reverse:
                hit = (col == row + lag) & (row % SUB + lag < SUB)
            else:
                hit = (col == row - lag) & (row % SUB >= lag)
            diag = diag + jnp.where(hit, r, 0.0)
        scores = (a_ref[...] + diag).astype(bf16)
        o_ref[:, vs] = o + jnp.dot(scores, v, preferred_element_type=f32)

    @pl.when(is_last & is_ctx)
    def _():
        fin_ref[0] = st_ref[...]


def _gla_direction(proj, logg, state_gla, layer, reverse):
    dirn = 1 if reverse else 0
    seq = lambda j: _gla_seq_of_chunk(cidx(j))
    cidx = (lambda j: GLA_CHUNKS - 1 - j) if reverse else (lambda j: j)
    q_blk = (S5_WIDTH) // GLA_QK
    v_blk = (S5_WIDTH + 2 * GLA_QK) // GLA_V
    return pl.pallas_call(
        functools.partial(_gla_kernel, reverse=reverse),
        out_shape=(jax.ShapeDtypeStruct((NT, GLA_V), f32),
                   jax.ShapeDtypeStruct((BATCH, GLA_HEADS, GLA_DK, GLA_DV), f32)),
        grid=(GLA_CHUNKS,),
        in_specs=[
            pl.BlockSpec((GLA_C, GLA_QK), lambda j: (cidx(j), q_blk)),
            pl.BlockSpec((GLA_C, GLA_QK), lambda j: (cidx(j), q_blk + 1)),
            pl.BlockSpec((GLA_C, GLA_V), lambda j: (cidx(j), v_blk)),
            pl.BlockSpec((GLA_C, GLA_QK), lambda j: (cidx(j), dirn)),
            pl.BlockSpec((1, 1, 1, GLA_HEADS, GLA_DK, GLA_DV),
                         lambda j: (jnp.maximum(seq(j) - BATCH, 0), layer, dirn, 0, 0, 0)),
        ],
        out_specs=(pl.BlockSpec((GLA_C, GLA_V), lambda j: (cidx(j), 0)),
                   pl.BlockSpec((1, GLA_HEADS, GLA_DK, GLA_DV), lambda j: (jnp.minimum(seq(j), BATCH - 1), 0, 0, 0))),
        scratch_shapes=[pltpu.VMEM((GLA_HEADS, GLA_DK, GLA_DV), f32), pltpu.VMEM((GLA_C, GLA_C), f32)],
        compiler_params=_cparams(("arbitrary",)),
        name="gla_bwd" if reverse else "gla_fwd",
    )(proj, proj, proj, logg, state_gla)


def _post_kernel(y_ref, of_ref, ob_ref, r_ref, ga_ref, gb_ref, x_ref, mod_ref,
                 wglu_ref, bglu_ref, ws5_ref, gn_ref, wgla_ref, wout_ref, n2_ref, wr_ref, br_ref,
                 x1_ref, h2_ref, gate_ref, idx_ref):
    g = jax.nn.gelu(y_ref[...].astype(f32))
    glu = jnp.dot(g.astype(bf16), wglu_ref[...], preferred_element_type=f32) + bglu_ref[...]
    ya = g * jax.nn.sigmoid(glu)

    o = of_ref[...] + ob_ref[...]
    heads = []
    for h in range(GLA_HEADS):
        oh = o[:, h * GLA_DV:(h + 1) * GLA_DV]
        heads.append(oh * lax.rsqrt(jnp.mean(oh * oh, axis=-1, keepdims=True) + EPS))
    r = r_ref[...].astype(f32)
    yb = jnp.concatenate(heads, axis=1) * gn_ref[...] * (r * jax.nn.sigmoid(r))

    merged = (jax.nn.sigmoid(ga_ref[...].astype(f32)) * jnp.dot(ya.astype(bf16), ws5_ref[...], preferred_element_type=f32)
              + jax.nn.sigmoid(gb_ref[...].astype(f32)) * jnp.dot(yb.astype(bf16), wgla_ref[...], preferred_element_type=f32))
    x1 = x_ref[...] + mod_ref[0, 2:3, :] * jnp.dot(merged.astype(bf16), wout_ref[...], preferred_element_type=f32)
    x1_ref[...] = x1

    y2 = x1 * lax.rsqrt(jnp.mean(x1 * x1, axis=-1, keepdims=True) + EPS) * n2_ref[...]
    h2 = y2 * (1.0 + mod_ref[0, 4:5, :]) + mod_ref[0, 3:4, :]
    h2_ref[...] = h2

    logits = jnp.dot(h2, wr_ref[...], precision=HIGHEST, preferred_element_type=f32) + br_ref[...]
    lane = lax.broadcasted_iota(i32, logits.shape, 1)
    lane_f = lane.astype(f32)
    vals = logits
    top_v, top_i = [], []
    for _ in range(TOP_K):
        m = jnp.max(vals, axis=-1, keepdims=True)
        idx = jnp.min(jnp.where(vals == m, lane_f, float(LANES)), axis=-1, keepdims=True).astype(i32)
        top_v.append(m)
        top_i.append(idx)
        vals = jnp.where(lane == idx, -jnp.inf, vals)
    ex = [jnp.exp(v - top_v[0]) for v in top_v]
    inv = 1.0 / (ex[0] + ex[1] + ex[2] + ex[3])
    gates = jnp.zeros(logits.shape, f32)
    idxs = jnp.zeros(logits.shape, i32)
    for kk in range(TOP_K):
        gates = jnp.where(lane == kk, ex[kk] * inv, gates)
        idxs = jnp.where(lane == kk, top_i[kk], idxs)
    gate_ref[...] = gates
    idx_ref[...] = idxs


def _post(y_s5, o_f, o_b, proj, x, mod_l, wglu, bglu, ws5, gnorm, wgla, wout, norm2, wr, br):
    const = lambda shape: pl.BlockSpec(shape, lambda i: (0,) * len(shape), pipeline_mode=pl.Buffered(1))
    r_blk = (S5_WIDTH + 2 * GLA_QK + GLA_V) // GLA_V
    ga_blk = 4096 // D_MODEL
    return pl.pallas_call(
        _post_kernel,
        out_shape=(jax.ShapeDtypeStruct((NT, D_MODEL), f32),
                   jax.ShapeDtypeStruct((NT, D_MODEL), f32),
                   jax.ShapeDtypeStruct((NT, LANES), f32),
                   jax.ShapeDtypeStruct((NT, LANES), i32)),
        grid=(NT // POST_TM,),
        in_specs=[
            pl.BlockSpec((POST_TM, S5_WIDTH), lambda i: (i, 0)),
            pl.BlockSpec((POST_TM, GLA_V), lambda i: (i, 0)),
            pl.BlockSpec((POST_TM, GLA_V), lambda i: (i, 0)),
            pl.BlockSpec((POST_TM, GLA_V), lambda i: (i, r_blk)),
            pl.BlockSpec((POST_TM, D_MODEL), lambda i: (i, ga_blk)),
            pl.BlockSpec((POST_TM, D_MODEL), lambda i: (i, ga_blk + 1)),
            pl.BlockSpec((POST_TM, D_MODEL), lambda i: (i, 0)),
            pl.BlockSpec((1, 6, D_MODEL), lambda i: (_row_seq_of_tile(i, POST_TM), 0, 0)),
            const((S5_WIDTH, S5_WIDTH)), const((1, S5_WIDTH)), const((S5_WIDTH, D_MODEL)),
            const((1, GLA_V)), const((GLA_V, D_MODEL)), const((D_MODEL, D_MODEL)), const((1, D_MODEL)),
            const((D_MODEL, LANES)), const((1, LANES)),
        ],
        out_specs=(pl.BlockSpec((POST_TM, D_MODEL), lambda i: (i, 0)),
                   pl.BlockSpec((POST_TM, D_MODEL), lambda i: (i, 0)),
                   pl.BlockSpec((POST_TM, LANES), lambda i: (i, 0)),
                   pl.BlockSpec((POST_TM, LANES), lambda i: (i, 0))),
        compiler_params=_cparams(("arbitrary",)),
        name="post",
    )(y_s5, o_f, o_b, proj, proj, proj, x, mod_l, wglu, bglu, ws5, gnorm, wgla, wout, norm2, wr, br)


def _routing(top_idx):
    m = NT * TOP_K
    e_flat = top_idx.reshape(-1)
    experts = jnp.arange(N_EXPERTS, dtype=i32)
    onehot = (e_flat[:, None] == experts[None, :]).astype(i32)
    csum = jnp.cumsum(onehot, axis=0)
    counts = csum[-1]
    starts = jnp.cumsum(counts) - counts
    nblk = (counts + MOE_BM - 1) // MOE_BM
    blk_end = jnp.cumsum(nblk)
    blk_start = blk_end - nblk
    dest = jnp.sum(onehot * (blk_start[None, :] * MOE_BM + csum - 1), axis=1).astype(i32)

    order = jnp.argsort(e_flat, stable=True).astype(i32)
    p = jnp.arange(MOE_ROWS, dtype=i32)
    e_p = jnp.minimum(jnp.sum((blk_end[None, :] <= (p // MOE_BM)[:, None]).astype(i32), axis=1), N_EXPERTS - 1)
    oh_p = (e_p[:, None] == experts[None, :]).astype(i32)
    pick = lambda v: jnp.sum(oh_p * v[None, :], axis=1)
    idx_in = p - pick(blk_start) * MOE_BM
    valid = (p < blk_end[-1] * MOE_BM) & (idx_in < pick(counts))
    src = order[jnp.clip(pick(starts) + idx_in, 0, m - 1)] // TOP_K
    row_tok = jnp.where(valid, src, 0).astype(i32)

    n_units = (nblk + MOE_RB - 1) // MOE_RB
    unit_end = jnp.cumsum(n_units)
    unit_start = unit_end - n_units
    total_units = unit_end[-1]
    u = jnp.arange(MOE_UNITS, dtype=i32)
    used = u < total_units
    e_of_u = jnp.minimum(jnp.sum((unit_end[None, :] <= u[:, None]).astype(i32), axis=1), N_EXPERTS - 1)
    local = u - unit_start[e_of_u]
    u_blk0 = blk_start[e_of_u] + local * MOE_RB
    u_nblk = jnp.minimum(MOE_RB, nblk[e_of_u] - local * MOE_RB)
    last_e = e_of_u[jnp.maximum(total_units - 1, 0)]
    u_exp = jnp.where(used, e_of_u, last_e).astype(i32)
    u_blk0 = jnp.where(used, u_blk0, 0).astype(i32)
    u_nblk = jnp.where(used, u_nblk, 0).astype(i32)
    return row_tok, dest, u_exp, u_blk0, u_nblk, blk_end[-1:].astype(i32)


def _gather_kernel(tok_ref, src_ref, o_ref, stage_ref, sem):
    i = pl.program_id(0)
    S = GATHER_SPLIT

    def row_copy(t, slot, r):
        return pltpu.make_async_copy(src_ref.at[pl.ds(pl.multiple_of(t * S, S), S), :],
                                     stage_ref.at[slot, pl.ds(pl.multiple_of(r * S, S), S), :], sem.at[slot])

    def issue(step, slot):
        base = step * GATHER_GB
        lax.fori_loop(0, GATHER_GB, lambda r, c: (row_copy(tok_ref[base + r], slot, r).start(), c)[1], 0, unroll=16)

    @pl.when(i == 0)
    def _():
        issue(0, 0)

    @pl.when(i + 1 < pl.num_programs(0))
    def _():
        issue(i + 1, (i + 1) % 2)

    def finish(slot):
        lax.fori_loop(0, GATHER_GB, lambda r, c: (row_copy(0, slot, r).wait(), c)[1], 0, unroll=32)
        parts = [stage_ref[slot, pl.ds(k, GATHER_GB, stride=S), :] for k in range(S)]
        o_ref[...] = jnp.concatenate(parts, axis=1).astype(bf16)

    for slot in range(2):
        pl.when(i % 2 == slot)(functools.partial(finish, slot))


def _gather_rows(h2, row_tok):
    S = GATHER_SPLIT
    return pl.pallas_call(
        _gather_kernel,
        out_shape=jax.ShapeDtypeStruct((MOE_ROWS, D_MODEL), bf16),
        grid_spec=pltpu.PrefetchScalarGridSpec(
            num_scalar_prefetch=1,
            grid=(MOE_ROWS // GATHER_GB,),
            in_specs=[pl.BlockSpec(memory_space=pl.ANY)],
            out_specs=pl.BlockSpec((GATHER_GB, D_MODEL), lambda i, tok: (i, 0)),
            scratch_shapes=[pltpu.VMEM((2, GATHER_GB * S, D_MODEL // S), f32), pltpu.SemaphoreType.DMA((2,))],
        ),
        compiler_params=_cparams(("arbitrary",)),
        name="moe_gather",
    )(row_tok, h2.reshape(NT * S, D_MODEL // S))


def _deinterleave_matrix():
    half = MOE_TA // 2
    src = jnp.arange(MOE_TA)[:, None]
    dst = jnp.arange(MOE_TA)[None, :]
    return (dst == (src % 2) * half + src // 2).astype(bf16)


def _moe_kernel(uexp_ref, ublk_ref, unb_ref, nused_ref, x_hbm, w1_ref, b1_ref, perm_ref, w2_ref, b2_ref,
                y_hbm, x_buf, act_buf, w_buf, y_buf, zero_buf, sem_in, sem_out):
    u = pl.program_id(0)
    j = pl.program_id(1)
    nb = unb_ref[u]
    row0 = ublk_ref[u] * MOE_BM
    big = MOE_QB * MOE_BM
    n_big = nb // MOE_QB
    n_small = nb - n_big * MOE_QB
    small0 = n_big * big

    def in_copy(r):
        rows = pl.ds(pl.multiple_of(r * MOE_BM, MOE_BM), MOE_BM)
        src = x_hbm.at[pl.ds(pl.multiple_of(row0 + r * MOE_BM, MOE_BM), MOE_BM), :]
        return pltpu.make_async_copy(src, x_buf.at[rows, :], sem_in)

    @pl.when((j == 0) & (nb > 0))
    def _():
        lax.fori_loop(0, nb, lambda r, c: (in_copy(r).start(), c)[1], 0)
        lax.fori_loop(0, nb, lambda r, c: (in_copy(r).wait(), c)[1], 0)

    @pl.when((j < MOE_J1) & (nb > 0))
    def _():
        w_buf[...] = w1_ref[0, 0].astype(bf16)
        half = MOE_TA // 2

        def act_rows(start, m):
            rows = pl.ds(pl.multiple_of(start, MOE_BM), m)
            h = jnp.dot(x_buf[rows, :], w_buf[...], preferred_element_type=f32) + b1_ref[0, 0]
            hb = h.astype(bf16)
            parts = []
            for s in range(MOE_TN // MOE_TA):
                hp = jnp.dot(hb[:, s * MOE_TA:(s + 1) * MOE_TA], perm_ref[...], preferred_element_type=f32)
                hg = jnp.minimum(hp[:, :half], SWIGLU_LIMIT)
                hl = jnp.clip(hp[:, half:], -SWIGLU_LIMIT, SWIGLU_LIMIT)
                parts.append(hg * jax.nn.sigmoid(SWIGLU_ALPHA * hg) * (hl + 1.0))
            act_buf[j, rows, :] = jnp.concatenate(parts, axis=1).astype(bf16)

        lax.fori_loop(0, n_big, lambda q, c: (act_rows(q * big, big), c)[1], 0)
        lax.fori_loop(0, n_small, lambda r, c: (act_rows(small0 + r * MOE_BM, MOE_BM), c)[1], 0)

    @pl.when((j >= MOE_J1) & (nb > 0))
    def _():
        w_buf[...] = w2_ref[0, 0].astype(bf16)
        col0 = pl.multiple_of((j - MOE_J1) * MOE_TN, MOE_TN)

        def out_copy(start, m):
            rows = pl.ds(pl.multiple_of(start, MOE_BM), m)
            dst = y_hbm.at[pl.ds(pl.multiple_of(row0 + start, MOE_BM), m), pl.ds(col0, MOE_TN)]
            return pltpu.make_async_copy(y_buf.at[rows, :], dst, sem_out)

        def out_rows(start, m):
            rows = pl.ds(pl.multiple_of(start, MOE_BM), m)
            acc = b2_ref[0, 0] + jnp.zeros((m, MOE_TN), f32)
            for jj in range(MOE_J1):
                acc = acc + jnp.dot(act_buf[jj, rows, :], w_buf[jj * MOE_TA:(jj + 1) * MOE_TA, :],
                                    preferred_element_type=f32)
            y_buf[rows, :] = acc
            out_copy(start, m).start()

        lax.fori_loop(0, n_big, lambda q, c: (out_rows(q * big, big), c)[1], 0)
        lax.fori_loop(0, n_small, lambda r, c: (out_rows(small0 + r * MOE_BM, MOE_BM), c)[1], 0)
        lax.fori_loop(0, n_big, lambda q, c: (out_copy(q * big, big).wait(), c)[1], 0)
        lax.fori_loop(0, n_small, lambda r, c: (out_copy(small0 + r * MOE_BM, MOE_BM).wait(), c)[1], 0)

    @pl.when((u == MOE_UNITS - 1) & (j == MOE_J1 + MOE_J2 - 1))
    def _():
        zero_buf[...] = jnp.zeros(zero_buf.shape, f32)

        def fill_copy(r):
            dst = y_hbm.at[pl.ds(pl.multiple_of(r * MOE_BM, MOE_BM), MOE_BM), :]
            return pltpu.make_async_copy(zero_buf, dst, sem_out)

        lax.fori_loop(nused_ref[0], MOE_BLOCKS, lambda r, c: (fill_copy(r).start(), c)[1], 0)
        lax.fori_loop(nused_ref[0], MOE_BLOCKS, lambda r, c: (fill_copy(r).wait(), c)[1], 0)


def _moe_experts(x_sorted, u_exp, u_blk0, u_nblk, n_used, w1, b1, w2, b2, layer):
    def w1_idx(u, j, uexp, ublk, unb, nused):
        return (layer, uexp[u], 0, jnp.where(unb[u] > 0, jnp.minimum(j, MOE_J1 - 1), MOE_J1 - 1))

    def w2_idx(u, j, uexp, ublk, unb, nused):
        return (layer, uexp[u], 0, jnp.where(unb[u] > 0, jnp.maximum(j - MOE_J1, 0), MOE_J2 - 1))

    return pl.pallas_call(
        _moe_kernel,
        out_shape=jax.ShapeDtypeStruct((MOE_ROWS, D_MODEL), f32),
        grid_spec=pltpu.PrefetchScalarGridSpec(
            num_scalar_prefetch=4,
            grid=(MOE_UNITS, MOE_J1 + MOE_J2),
            in_specs=[
                pl.BlockSpec(memory_space=pl.ANY),
                pl.BlockSpec((1, 1, D_MODEL, MOE_TN), w1_idx),
                pl.BlockSpec((1, 1, 1, MOE_TN), w1_idx),
                pl.BlockSpec((MOE_TA, MOE_TA), lambda u, j, *_: (0, 0)),
                pl.BlockSpec((1, 1, D_MODEL, MOE_TN), w2_idx),
                pl.BlockSpec((1, 1, 1, MOE_TN), w2_idx),
            ],
            out_specs=pl.BlockSpec(memory_space=pl.ANY),
            scratch_shapes=[
                pltpu.VMEM((MOE_RMAX, D_MODEL), bf16),
                pltpu.VMEM((MOE_J1, MOE_RMAX, MOE_TA), bf16),
                pltpu.VMEM((D_MODEL, MOE_TN), bf16),
                pltpu.VMEM((MOE_RMAX, MOE_TN), f32),
                pltpu.VMEM((MOE_BM, D_MODEL), f32),
                pltpu.SemaphoreType.DMA(()),
                pltpu.SemaphoreType.DMA(()),
            ],
        ),
        compiler_params=_cparams(("arbitrary", "arbitrary")),
        name="moe_experts",
    )(u_exp, u_blk0, u_nblk, n_used, x_sorted, w1, b1, _deinterleave_matrix(), w2, b2)


def _combine_kernel(dest_ref, y_hbm, gate_ref, x1_ref, mod_ref, nf_ref, o_ref, y_buf, sem, *, final_norm):
    i = pl.program_id(0)

    def row_copy(p, slot, kk, r):
        return pltpu.make_async_copy(y_hbm.at[pl.ds(p, 1), :], y_buf.at[slot, kk, pl.ds(r, 1), :], sem.at[slot])

    def issue(step, slot):
        base = step * COMB_TM * TOP_K

        def body(r, c):
            for kk in range(TOP_K):
                row_copy(dest_ref[base + r * TOP_K + kk], slot, kk, r).start()
            return c

        lax.fori_loop(0, COMB_TM, body, 0, unroll=4)

    @pl.when(i == 0)
    def _():
        issue(0, 0)

    @pl.when(i + 1 < pl.num_programs(0))
    def _():
        issue(i + 1, (i + 1) % 2)

    def finish(slot):
        def wait(r, c):
            for kk in range(TOP_K):
                row_copy(0, slot, kk, r).wait()
            return c

        lax.fori_loop(0, COMB_TM, wait, 0, unroll=8)
        gates = gate_ref[...]
        acc = jnp.zeros((COMB_TM, D_MODEL), f32)
        for kk in range(TOP_K):
            acc = acc + gates[:, kk:kk + 1] * y_buf[slot, kk]
        x2 = x1_ref[...] + mod_ref[0, 5:6, :] * acc
        if final_norm:
            x2 = x2 * lax.rsqrt(jnp.mean(x2 * x2, axis=-1, keepdims=True) + EPS) * nf_ref[...]
        o_ref[...] = x2

    for slot in range(2):
        pl.when(i % 2 == slot)(functools.partial(finish, slot))


def _combine(y_sorted, dest, gates, x1, mod_l, norm_f, final_norm):
    return pl.pallas_call(
        functools.partial(_combine_kernel, final_norm=final_norm),
        out_shape=jax.ShapeDtypeStruct((NT, D_MODEL), f32),
        grid_spec=pltpu.PrefetchScalarGridSpec(
            num_scalar_prefetch=1,
            grid=(NT // COMB_TM,),
            in_specs=[
                pl.BlockSpec(memory_space=pl.ANY),
                pl.BlockSpec((COMB_TM, LANES), lambda i, d: (i, 0)),
                pl.BlockSpec((COMB_TM, D_MODEL), lambda i, d: (i, 0)),
                pl.BlockSpec((1, 6, D_MODEL), lambda i, d: (_row_seq_of_tile(i, COMB_TM), 0, 0)),
                pl.BlockSpec((1, D_MODEL), lambda i, d: (0, 0)),
            ],
            out_specs=pl.BlockSpec((COMB_TM, D_MODEL), lambda i, d: (i, 0)),
            scratch_shapes=[pltpu.VMEM((2, TOP_K, COMB_TM, D_MODEL), f32), pltpu.SemaphoreType.DMA((2,))],
        ),
        compiler_params=_cparams(("arbitrary",)),
        name="moe_combine_final" if final_norm else "moe_combine",
    )(dest, y_sorted, gates, x1, mod_l, norm_f)


def kernel(x_prompt, x_sample, c, state_s5_re, state_s5_im, state_gla, c_ctx, w_mod, b_mod, norm1, w_in, s5_lam_re, s5_lam_im, s5_log_dt, s5_b_re, s5_b_im, s5_c_re, s5_c_im, s5_d, s5_w_glu, s5_b_glu, w_s5_out, gla_w_g2, gla_b_g, gla_norm, w_gla_out, w_out, norm2, w_router, b_router, w_e1, b_e1, w_e2, b_e2, norm_f):
    cond = jnp.concatenate([c_ctx[None], c, jnp.zeros((N_MODROWS - 1 - DEC_BATCH, D_MODEL), f32)], axis=0)
    mod = _modulation(cond, w_mod, b_mod).reshape(DEPTH, N_MODROWS, 6, D_MODEL)
    x = _embed(x_prompt, x_sample, _grid_pos_embed())

    glr0 = S5_WIDTH + 2 * GLA_QK + 2 * GLA_V
    glr1 = glr0 + N_DIRS * GLA_RANK
    s5_re_out, s5_im_out, gla_out = [], [], []
    toeplitz, w_state, w_carry, coef = jax.vmap(_s5_weights)(
        s5_lam_re, s5_lam_im, s5_log_dt, s5_b_re, s5_b_im, s5_c_re, s5_c_im, s5_d)
    b_e1_r = b_e1.reshape(DEPTH, N_EXPERTS, 1, 2 * D_MODEL)
    b_e2_r = b_e2.reshape(DEPTH, N_EXPERTS, 1, D_MODEL)
    for l in range(DEPTH):
        mod_l = mod[l]
        w_main = jnp.concatenate([w_in[l, :, :glr0], w_in[l, :, glr1:]], axis=1).astype(bf16)
        w_glr = w_in[l, :, glr0:glr1].astype(bf16)
        zg = jnp.zeros((GLA_RANK, GLA_QK), f32)
        w_g2bd = jnp.concatenate([jnp.concatenate([gla_w_g2[l, 0], zg], axis=1),
                                  jnp.concatenate([zg, gla_w_g2[l, 1]], axis=1)], axis=0)
        b_g = gla_b_g[l].reshape(1, N_DIRS * GLA_QK)
        h0 = _s5_init_rows(state_s5_re[:, l], state_s5_im[:, l])
        w_r = jnp.concatenate([w_router[l], jnp.zeros((D_MODEL, LANES - N_EXPERTS), f32)], axis=1)
        b_r = jnp.concatenate([b_router[l], jnp.full((LANES - N_EXPERTS,), -jnp.inf, f32)]).reshape(1, LANES)

        proj, logg = _pre(x, mod_l, norm1[l].reshape(1, D_MODEL), w_main, w_glr, w_g2bd, b_g)
        y_g, s5_fin = _s5_scan(_s5_to_groups(proj[:, :S5_WIDTH]), toeplitz[l], w_state[l], w_carry[l], coef[l], h0)
        y_s5 = _s5_from_groups(y_g)
        o_f, st_f = _gla_direction(proj, logg, state_gla, l, False)
        o_b, st_b = _gla_direction(proj, logg, state_gla, l, True)

        x1, h2, gates, top_idx = _post(
            y_s5, o_f, o_b, proj, x, mod_l,
            s5_w_glu[l].astype(bf16), s5_b_glu[l].reshape(1, S5_WIDTH), w_s5_out[l].astype(bf16),
            jnp.tile(gla_norm[l], GLA_HEADS).reshape(1, GLA_V), w_gla_out[l].astype(bf16), w_out[l].astype(bf16),
            norm2[l].reshape(1, D_MODEL), w_r, b_r)

        row_tok, dest, u_exp, u_blk0, u_nblk, n_used = _routing(top_idx[:, :TOP_K])
        x_sorted = _gather_rows(h2, row_tok)
        y_sorted = _moe_experts(x_sorted, u_exp, u_blk0, u_nblk, n_used, w_e1, b_e1_r, w_e2, b_e2_r, l)
        x = _combine(y_sorted, dest, gates, x1, mod_l, norm_f.reshape(1, D_MODEL), l == DEPTH - 1)

        P = S5_STATE
        fin = jnp.transpose(s5_fin, (1, 0, 2))
        s5_re_out.append(jnp.stack([fin[..., 0:P], fin[..., 2 * P:3 * P]], axis=1))
        s5_im_out.append(jnp.stack([fin[..., P:2 * P], fin[..., 3 * P:4 * P]], axis=1))
        gla_out.append(jnp.stack([st_f, st_b], axis=1))

    y_prompt = x[:N_CTX].reshape(BATCH, SEQ, D_MODEL)
    y_sample = x[N_CTX:].reshape(DEC_BATCH, DEC_SEQ, D_MODEL)
    return (y_prompt, y_sample, jnp.stack(s5_re_out, axis=1), jnp.stack(s5_im_out, axis=1),
            jnp.stack(gla_out, axis=1))
```

```python
import functools
import math

import jax
import jax.numpy as jnp
import numpy as np
from jax import lax
from jax.experimental import pallas as pl
from jax.experimental.pallas import tpu as pltpu

f32 = jnp.float32
bf16 = jnp.bfloat16
i32 = jnp.int32
HIGHEST = lax.Precision.HIGHEST

D_MODEL = 2048
BATCH = 16
SEQ = 256
DEPTH = 2
DEC_BATCH = 2
DEC_SEQ = 2048
GRID_W = 64
N_DIRS = 2
S5_WIDTH = 1024
S5_GROUP = 16
S5_GROUPS = 64
S5_STATE = 64
GLA_HEADS = 4
GLA_DK = 128
GLA_DV = 256
GLA_QK = 512
GLA_V = 1024
GLA_RANK = 16
GLA_GATE_NORM = 16.0
N_EXPERTS = 32
TOP_K = 4
SWIGLU_LIMIT = 7.0
SWIGLU_ALPHA = 1.702
POS_BASE = 10000.0
EPS = 1e-6

N_CTX = BATCH * SEQ
N_LAT = DEC_BATCH * DEC_SEQ
NT = N_CTX + N_LAT
N_SEQS = BATCH + DEC_BATCH
N_MODROWS = 8

LANES = 128
VMEM_LIMIT = 56 * 1024 * 1024

S5_T = 16
S5_ROWS = NT // S5_T
S5_CTX_CHUNKS = SEQ // S5_T
S5_LAT_CHUNKS = DEC_SEQ // S5_T
GLA_C = 128
GLA_SUB = 16
GLA_NSUB = GLA_C // GLA_SUB
GLA_CHUNKS = NT // GLA_C
GLA_CTX_CHUNKS = N_CTX // GLA_C
PRE_TM = 1024
PRE_TN = 1024
PROJ_COLS = 8192
POST_TM = 256
MOE_BM = 128
MOE_ROWS = NT * TOP_K + N_EXPERTS * MOE_BM
MOE_BLOCKS = MOE_ROWS // MOE_BM
MOE_RB = 16
MOE_QB = 4
MOE_RMAX = MOE_RB * MOE_BM
MOE_UNITS = N_EXPERTS + MOE_BLOCKS // MOE_RB
MOE_TN = 512
MOE_TA = MOE_TN // 2
MOE_J1 = 2 * D_MODEL // MOE_TN
MOE_J2 = D_MODEL // MOE_TN
GATHER_GB = 512
GATHER_SPLIT = D_MODEL // LANES
COMB_TM = 128


def _cparams(sem, **kw):
    return pltpu.CompilerParams(dimension_semantics=sem, vmem_limit_bytes=VMEM_LIMIT, **kw)


def _row_seq_of_tile(i, tile_rows):
    first = i * tile_rows
    return jnp.where(first < N_CTX, 0, 1 + (first - N_CTX) // DEC_SEQ)


def _log_sigmoid(x):
    return -(jnp.maximum(-x, 0.0) + jnp.log1p(jnp.exp(-jnp.abs(x))))


MOD_TN = 1024


def _mod_kernel(c_ref, w_ref, b_ref, o_ref):
    c = c_ref[...]
    s = (c * jax.nn.sigmoid(c)).astype(bf16)
    o_ref[0] = jnp.dot(s, w_ref[0].astype(bf16), preferred_element_type=f32) + b_ref[0]


def _modulation(cond, w_mod, b_mod):
    n_out = 6 * D_MODEL
    return pl.pallas_call(
        _mod_kernel,
        out_shape=jax.ShapeDtypeStruct((DEPTH, N_MODROWS, n_out), f32),
        grid=(DEPTH, n_out // MOD_TN),
        in_specs=[
            pl.BlockSpec((N_MODROWS, D_MODEL), lambda l, j: (0, 0)),
            pl.BlockSpec((1, D_MODEL, MOD_TN), lambda l, j: (l, 0, j)),
            pl.BlockSpec((1, 1, MOD_TN), lambda l, j: (l, 0, j)),
        ],
        out_specs=pl.BlockSpec((1, N_MODROWS, MOD_TN), lambda l, j: (l, 0, j)),
        compiler_params=_cparams(("arbitrary", "arbitrary")),
        name="modulation",
    )(cond, w_mod, b_mod.reshape(DEPTH, 1, n_out))


EMB_TM = 512


def _embed_kernel(xp_ref, xs_ref, pe_ref, o_ref):
    i = pl.program_id(0)

    @pl.when(i < N_CTX // EMB_TM)
    def _():
        o_ref[...] = xp_ref[...]

    @pl.when(i >= N_CTX // EMB_TM)
    def _():
        o_ref[...] = xs_ref[...] + pe_ref[...]


def _embed(x_prompt, x_sample, pe):
    nc = N_CTX // EMB_TM
    per_seq = DEC_SEQ // EMB_TM
    return pl.pallas_call(
        _embed_kernel,
        out_shape=jax.ShapeDtypeStruct((NT, D_MODEL), f32),
        grid=(NT // EMB_TM,),
        in_specs=[
            pl.BlockSpec((EMB_TM, D_MODEL), lambda i: (jnp.minimum(i, nc - 1), 0)),
            pl.BlockSpec((EMB_TM, D_MODEL), lambda i: (jnp.maximum(i - nc, 0), 0)),
            pl.BlockSpec((EMB_TM, D_MODEL), lambda i: (jnp.maximum(i - nc, 0) % per_seq, 0)),
        ],
        out_specs=pl.BlockSpec((EMB_TM, D_MODEL), lambda i: (i, 0)),
        compiler_params=_cparams(("arbitrary",)),
        name="embed",
    )(x_prompt.reshape(N_CTX, D_MODEL), x_sample.reshape(N_LAT, D_MODEL), pe)


def _grid_pos_embed():
    rows = DEC_SEQ // GRID_W
    rr, cc = jnp.meshgrid(jnp.arange(rows, dtype=f32), jnp.arange(GRID_W, dtype=f32), indexing="ij")
    quarter = D_MODEL // 4
    omega = 1.0 / (POS_BASE ** (jnp.arange(quarter, dtype=f32) / quarter))

    def emb(pos):
        ang = pos[:, None] * omega[None, :]
        return jnp.concatenate([jnp.sin(ang), jnp.cos(ang)], axis=-1)

    return jnp.concatenate([emb(rr.reshape(-1)), emb(cc.reshape(-1))], axis=-1)


def _pre_kernel(x_ref, mod_ref, g_ref, w_ref, wglr_ref, wg2_ref, bg_ref, proj_ref, logg_ref, h_ref):
    @pl.when(pl.program_id(1) == 0)
    def _():
        x = x_ref[...]
        y = x * lax.rsqrt(jnp.mean(x * x, axis=-1, keepdims=True) + EPS) * g_ref[...]
        h = y * (1.0 + mod_ref[0, 1:2, :]) + mod_ref[0, 0:1, :]
        hb = h.astype(bf16)
        h_ref[...] = hb
        glr = jnp.dot(hb, wglr_ref[...], preferred_element_type=f32)
        gate = jnp.dot(glr, wg2_ref[...], precision=HIGHEST, preferred_element_type=f32) + bg_ref[...]
        logg_ref[...] = _log_sigmoid(gate) * (1.0 / GLA_GATE_NORM)

    proj_ref[...] = jnp.dot(h_ref[...], w_ref[...], preferred_element_type=f32).astype(bf16)


def _pre(x, mod_l, norm1, w_main, w_glr, w_g2bd, b_g):
    return pl.pallas_call(
        _pre_kernel,
        out_shape=(jax.ShapeDtypeStruct((NT, PROJ_COLS), bf16),
                   jax.ShapeDtypeStruct((NT, N_DIRS * GLA_QK), f32)),
        grid=(NT // PRE_TM, PROJ_COLS // PRE_TN),
        in_specs=[
            pl.BlockSpec((PRE_TM, D_MODEL), lambda i, j: (i, 0)),
            pl.BlockSpec((1, 6, D_MODEL), lambda i, j: (_row_seq_of_tile(i, PRE_TM), 0, 0)),
            pl.BlockSpec((1, D_MODEL), lambda i, j: (0, 0)),
            pl.BlockSpec((D_MODEL, PRE_TN), lambda i, j: (0, j)),
            pl.BlockSpec((D_MODEL, N_DIRS * GLA_RANK), lambda i, j: (0, 0)),
            pl.BlockSpec((N_DIRS * GLA_RANK, N_DIRS * GLA_QK), lambda i, j: (0, 0)),
            pl.BlockSpec((1, N_DIRS * GLA_QK), lambda i, j: (0, 0)),
        ],
        out_specs=(pl.BlockSpec((PRE_TM, PRE_TN), lambda i, j: (i, j)),
                   pl.BlockSpec((PRE_TM, N_DIRS * GLA_QK), lambda i, j: (i, 0))),
        scratch_shapes=[pltpu.VMEM((PRE_TM, D_MODEL), bf16)],
        compiler_params=_cparams(("arbitrary", "arbitrary")),
        name="pre",
    )(x, mod_l, norm1, w_main, w_glr, w_g2bd, b_g)


def _s5_weights(lam_re, lam_im, log_dt, b_re, b_im, c_re, c_im, d):
    T, G, P, H = S5_T, S5_GROUPS, S5_STATE, S5_GROUP
    dt = jnp.exp(log_dt)[..., None]
    z_re, z_im = lam_re * dt, lam_im * dt
    mag = jnp.exp(z_re)
    a_re, a_im = mag * jnp.cos(z_im), mag * jnp.sin(z_im)
    den = lam_re * lam_re + lam_im * lam_im
    n_re, n_im = a_re - 1.0, a_im
    k_re = (n_re * lam_re + n_im * lam_im) / den
    k_im = (n_im * lam_re - n_re * lam_im) / den
    bb_re = k_re[..., None] * b_re - k_im[..., None] * b_im
    bb_im = k_re[..., None] * b_im + k_im[..., None] * b_re
    n = jnp.arange(T + 1, dtype=f32)[:, None, None, None]
    pmag = jnp.exp(n * z_re)
    pw_re, pw_im = pmag * jnp.cos(n * z_im), pmag * jnp.sin(n * z_im)
    abb_re = pw_re[..., None] * bb_re - pw_im[..., None] * bb_im
    abb_im = pw_re[..., None] * bb_im + pw_im[..., None] * bb_re
    kern = (jnp.einsum("dgop,ndgpi->ndgoi", c_re, abb_re[:T], precision=HIGHEST)
            - jnp.einsum("dgop,ndgpi->ndgoi", c_im, abb_im[:T], precision=HIGHEST))

    lag = np.arange(T)[None, :, None] - np.arange(T)[:, None, None]
    sel_f = jnp.asarray(lag == np.arange(T)[None, None, :], f32)
    sel_b = jnp.asarray(-lag == np.arange(T)[None, None, :], f32)
    kf = jnp.einsum("stn,ngoi->stgoi", sel_f, kern[:, 0], precision=HIGHEST)
    kb = jnp.einsum("stn,ngoi->stgoi", sel_b, kern[:, 1], precision=HIGHEST)
    eye_t = jnp.eye(T, dtype=f32)[:, :, None, None, None]
    eye_h = jnp.eye(H, dtype=f32)[None, None, None, :, :]
    dterm = eye_t * eye_h * d[None, None, :, :, None]
    toeplitz = jnp.transpose(kf + kb + dterm, (2, 0, 4, 1, 3)).reshape(G, T * H, T * H)

    def state_cols(powers, dirn):
        re = jnp.transpose(abb_re[powers, dirn], (1, 0, 3, 2)).reshape(G, T * H, P)
        im = jnp.transpose(abb_im[powers, dirn], (1, 0, 3, 2)).reshape(G, T * H, P)
        return [re, im, im, re]

    w_state = jnp.concatenate(state_cols(slice(T - 1, None, -1), 0) + state_cols(slice(0, T), 1), axis=-1)

    def carry_rows(powers, dirn):
        pr = pw_re[powers, dirn]
        pi = pw_im[powers, dirn]
        cr, ci = c_re[dirn], c_im[dirn]
        on_re = cr[None] * pr[:, :, None, :] - ci[None] * pi[:, :, None, :]
        on_im = -cr[None] * pi[:, :, None, :] - ci[None] * pr[:, :, None, :]
        to_cols = lambda m: jnp.transpose(m, (1, 3, 0, 2)).reshape(G, P, T * H)
        return [to_cols(on_re), to_cols(on_im)]

    w_carry = jnp.concatenate(carry_rows(slice(1, T + 1), 0) + carry_rows(slice(T, 0, -1), 1), axis=1)

    def coef_rows(dirn):
        ar, ai = pw_re[T, dirn], pw_im[T, dirn]
        return [jnp.concatenate([ar, ar], -1), jnp.concatenate([-ai, ai], -1), jnp.concatenate([ai, -ai], -1)]

    zero = jnp.zeros((G, 2 * P), f32)
    coef = jnp.stack(coef_rows(0) + coef_rows(1) + [zero, zero], axis=1)
    return toeplitz.astype(bf16), w_state.astype(bf16), w_carry.astype(bf16), coef


def _s5_kernel(u_ref, tz_ref, ws_ref, wc_ref, coef_ref, h0_ref, y_ref, fin_ref, s_ref, hin_ref):
    u = u_ref[0]
    s_ref[...] = jnp.dot(u, ws_ref[0], preferred_element_type=f32)
    co = coef_ref[0]
    a1f, a2f, a2sf = co[0:1], co[1:2], co[2:3]
    a1b, a2b, a2sb = co[3:4], co[4:5], co[5:6]
    P2 = 2 * S5_STATE

    def step(rf, rb, n_rows, carry):
        hf, hfs, hb, hbs = carry
        hin_ref[pl.ds(rf, n_rows), 0:P2] = hf
        hin_ref[pl.ds(rb, n_rows), P2:2 * P2] = hb
        sf = s_ref[pl.ds(rf, n_rows), 0:P2]
        sfs = s_ref[pl.ds(rf, n_rows), P2:2 * P2]
        sb = s_ref[pl.ds(rb, n_rows), 2 * P2:3 * P2]
        sbs = s_ref[pl.ds(rb, n_rows), 3 * P2:4 * P2]
        return (a1f * hf + a2f * hfs + sf, a1f * hfs + a2sf * hf + sfs,
                a1b * hb + a2b * hbs + sb, a1b * hbs + a2sb * hb + sbs)

    zero = jnp.zeros((BATCH, P2), f32)
    carry = (zero, zero, zero, zero)
    for c in range(S5_CTX_CHUNKS):
        carry = step(c * BATCH, (S5_CTX_CHUNKS - 1 - c) * BATCH, BATCH, carry)
    fin_ref[0] = jnp.concatenate([carry[0], carry[2]], axis=1)

    h0 = h0_ref[0]
    lat0 = S5_CTX_CHUNKS * BATCH

    carry = (h0[:, 0:P2], h0[:, P2:2 * P2], h0[:, 2 * P2:3 * P2], h0[:, 3 * P2:4 * P2])
    for c in range(S5_LAT_CHUNKS):
        carry = step(lat0 + c * DEC_BATCH, lat0 + (S5_LAT_CHUNKS - 1 - c) * DEC_BATCH, DEC_BATCH, carry)

    y = jnp.dot(u, tz_ref[0], preferred_element_type=f32)
    y = y + jnp.dot(hin_ref[...].astype(bf16), wc_ref[0], preferred_element_type=f32)
    y_ref[0] = y.astype(bf16)


def _s5_scan(u_g, toeplitz, w_state, w_carry, coef, h0):
    G, TH, P2 = S5_GROUPS, S5_T * S5_GROUP, 2 * S5_STATE
    return pl.pallas_call(
        _s5_kernel,
        out_shape=(jax.ShapeDtypeStruct((G, S5_ROWS, TH), bf16),
                   jax.ShapeDtypeStruct((G, BATCH, 2 * P2), f32)),
        grid=(G,),
        in_specs=[
            pl.BlockSpec((1, S5_ROWS, TH), lambda g: (g, 0, 0)),
            pl.BlockSpec((1, TH, TH), lambda g: (g, 0, 0)),
            pl.BlockSpec((1, TH, 4 * P2), lambda g: (g, 0, 0)),
            pl.BlockSpec((1, 2 * P2, TH), lambda g: (g, 0, 0)),
            pl.BlockSpec((1, 8, P2), lambda g: (g, 0, 0)),
            pl.BlockSpec((1, DEC_BATCH, 4 * P2), lambda g: (g, 0, 0)),
        ],
        out_specs=(pl.BlockSpec((1, S5_ROWS, TH), lambda g: (g, 0, 0)),
                   pl.BlockSpec((1, BATCH, 2 * P2), lambda g: (g, 0, 0))),
        scratch_shapes=[pltpu.VMEM((S5_ROWS, 4 * P2), f32), pltpu.VMEM((S5_ROWS, 2 * P2), f32)],
        compiler_params=_cparams(("arbitrary",)),
        name="s5_scan",
    )(u_g, toeplitz, w_state, w_carry, coef, h0)


def _s5_to_groups(u):
    G, H, T = S5_GROUPS, S5_GROUP, S5_T
    ctx = u[:N_CTX].reshape(BATCH, S5_CTX_CHUNKS, T, G, H)
    lat = u[N_CTX:].reshape(DEC_BATCH, S5_LAT_CHUNKS, T, G, H)
    ctx = jnp.transpose(ctx, (3, 1, 0, 2, 4)).reshape(G, S5_CTX_CHUNKS * BATCH, T * H)
    lat = jnp.transpose(lat, (3, 1, 0, 2, 4)).reshape(G, S5_LAT_CHUNKS * DEC_BATCH, T * H)
    return jnp.concatenate([ctx, lat], axis=1)


def _s5_from_groups(y):
    G, H, T = S5_GROUPS, S5_GROUP, S5_T
    nc = S5_CTX_CHUNKS * BATCH
    ctx = y[:, :nc].reshape(G, S5_CTX_CHUNKS, BATCH, T, H)
    lat = y[:, nc:].reshape(G, S5_LAT_CHUNKS, DEC_BATCH, T, H)
    ctx = jnp.transpose(ctx, (2, 1, 3, 0, 4)).reshape(N_CTX, G * H)
    lat = jnp.transpose(lat, (2, 1, 3, 0, 4)).reshape(N_LAT, G * H)
    return jnp.concatenate([ctx, lat], axis=0)


def _s5_init_rows(s_re, s_im):
    parts = [s_re[:, 0], s_im[:, 0], s_im[:, 0], s_re[:, 0], s_re[:, 1], s_im[:, 1], s_im[:, 1], s_re[:, 1]]
    return jnp.transpose(jnp.concatenate(parts, axis=-1), (1, 0, 2))


def _gla_seq_of_chunk(cj):
    per_ctx = SEQ // GLA_C
    per_lat = DEC_SEQ // GLA_C
    return jnp.where(cj < GLA_CTX_CHUNKS, cj // per_ctx, BATCH + (cj - GLA_CTX_CHUNKS) // per_lat)


def _gla_kernel(q_ref, k_ref, v_ref, lg_ref, s0_ref, o_ref, fin_ref, st_ref, a_ref, *, reverse):
    j = pl.program_id(0)
    cj = (GLA_CHUNKS - 1 - j) if reverse else j
    per_ctx = SEQ // GLA_C
    per_lat = DEC_SEQ // GLA_C
    pos = jnp.where(cj < GLA_CTX_CHUNKS, cj % per_ctx, (cj - GLA_CTX_CHUNKS) % per_lat)
    n_in_seq = jnp.where(cj < GLA_CTX_CHUNKS, per_ctx, per_lat)
    is_first = (pos == n_in_seq - 1) if reverse else (pos == 0)
    is_last = (pos == 0) if reverse else (pos == n_in_seq - 1)
    is_ctx = cj < GLA_CTX_CHUNKS

    @pl.when(is_first & is_ctx)
    def _():
        st_ref[...] = jnp.zeros(st_ref.shape, f32)

    @pl.when(is_first & jnp.logical_not(is_ctx))
    def _():
        st_ref[...] = s0_ref[0, 0, 0]

    C, SUB = GLA_C, GLA_SUB
    row = lax.broadcasted_iota(i32, (C, C), 0)
    col = lax.broadcasted_iota(i32, (C, C), 1)
    if reverse:
        m_cum = (col >= row)
        m_ref = (col >= (row // SUB + 1) * SUB)
    else:
        m_cum = (col <= row)
        m_ref = (col < (row // SUB) * SUB)
    m_both = jnp.concatenate([m_cum.astype(f32), m_ref.astype(f32)], axis=0)
    lg_all = lg_ref[...]
    sums = jnp.dot(m_both, lg_all, precision=HIGHEST, preferred_element_type=f32)
    scale = GLA_DK ** -0.5
    a_ref[...] = jnp.zeros((C, C), f32)

    for h in range(GLA_HEADS):
        ks = slice(h * GLA_DK, (h + 1) * GLA_DK)
        vs = slice(h * GLA_DV, (h + 1) * GLA_DV)
        lg = lg_all[:, ks]
        cum = sums[0:C, ks]
        ref = sums[C:2 * C, ks]
        tot = jnp.sum(lg, axis=0, keepdims=True)
        q = q_ref[:, ks].astype(f32) * scale
        k = k_ref[:, ks].astype(f32)
        v = v_ref[:, vs]
        state = st_ref[h]

        q_in = (q * jnp.exp(cum)).astype(bf16)
        o = jnp.dot(q_in, state.astype(bf16), preferred_element_type=f32)
        k_st = k * jnp.exp(tot - cum)
        tot_col = jnp.sum(lg.T, axis=1, keepdims=True)
        st_ref[h] = jnp.exp(tot_col) * state + jnp.dot(k_st.T.astype(bf16), v, preferred_element_type=f32)

        qt = (q * jnp.exp(cum - ref)).astype(bf16)
        for blk in range(GLA_NSUB):
            rows = slice(blk * SUB, (blk + 1) * SUB)
            cols = slice((blk + 1) * SUB, C) if reverse else slice(0, blk * SUB)
            if cols.start == cols.stop:
                continue
            kt = (k[cols] * jnp.exp(ref[blk * SUB:blk * SUB + 1] - cum[cols])).astype(bf16)
            a_ref[rows, cols] = lax.dot_general(qt[rows], kt, (((1,), (1,)), ((), ())), preferred_element_type=f32)

        diag = jnp.zeros((C, C), f32)
        for lag in range(SUB):
            shift = (C - lag) % C if reverse else lag
            k_sh = k if lag == 0 else pltpu.roll(k, shift, axis=0)
            c_sh = cum if lag == 0 else pltpu.roll(cum, shift, axis=0)
            z = q * k_sh * jnp.exp(jnp.minimum(cum - c_sh, 0.0))
            r = jnp.sum(z, axis=1, keepdims=True)
            if reverse:
                hit = (col == row + lag) & (row % SUB + lag < SUB)
            else:
                hit = (col == row - lag) & (row % SUB >= lag)
            diag = diag + jnp.where(hit, r, 0.0)
        scores = (a_ref[...] + diag).astype(bf16)
        o_ref[:, vs] = o + jnp.dot(scores, v, preferred_element_type=f32)

    @pl.when(is_last & is_ctx)
    def _():
        fin_ref[0] = st_ref[...]


def _gla_direction(proj, logg, state_gla, layer, reverse):
    dirn = 1 if reverse else 0
    seq = lambda j: _gla_seq_of_chunk(cidx(j))
    cidx = (lambda j: GLA_CHUNKS - 1 - j) if reverse else (lambda j: j)
    q_blk = (S5_WIDTH) // GLA_QK
    v_blk = (S5_WIDTH + 2 * GLA_QK) // GLA_V
    return pl.pallas_call(
        functools.partial(_gla_kernel, reverse=reverse),
        out_shape=(jax.ShapeDtypeStruct((NT, GLA_V), f32),
                   jax.ShapeDtypeStruct((BATCH, GLA_HEADS, GLA_DK, GLA_DV), f32)),
        grid=(GLA_CHUNKS,),
        in_specs=[
            pl.BlockSpec((GLA_C, GLA_QK), lambda j: (cidx(j), q_blk)),
            pl.BlockSpec((GLA_C, GLA_QK), lambda j: (cidx(j), q_blk + 1)),
            pl.BlockSpec((GLA_C, GLA_V), lambda j: (cidx(j), v_blk)),
            pl.BlockSpec((GLA_C, GLA_QK), lambda j: (cidx(j), dirn)),
            pl.BlockSpec((1, 1, 1, GLA_HEADS, GLA_DK, GLA_DV),
                         lambda j: (jnp.maximum(seq(j) - BATCH, 0), layer, dirn, 0, 0, 0)),
        ],
        out_specs=(pl.BlockSpec((GLA_C, GLA_V), lambda j: (cidx(j), 0)),
                   pl.BlockSpec((1, GLA_HEADS, GLA_DK, GLA_DV), lambda j: (jnp.minimum(seq(j), BATCH - 1), 0, 0, 0))),
        scratch_shapes=[pltpu.VMEM((GLA_HEADS, GLA_DK, GLA_DV), f32), pltpu.VMEM((GLA_C, GLA_C), f32)],
        compiler_params=_cparams(("arbitrary",)),
        name="gla_bwd" if reverse else "gla_fwd",
    )(proj, proj, proj, logg, state_gla)


def _post_kernel(y_ref, of_ref, ob_ref, r_ref, ga_ref, gb_ref, x_ref, mod_ref,
                 wglu_ref, bglu_ref, ws5_ref, gn_ref, wgla_ref, wout_ref, n2_ref, wr_ref, br_ref,
                 x1_ref, h2_ref, gate_ref, idx_ref):
    g = jax.nn.gelu(y_ref[...].astype(f32))
    glu = jnp.dot(g.astype(bf16), wglu_ref[...], preferred_element_type=f32) + bglu_ref[...]
    ya = g * jax.nn.sigmoid(glu)

    o = of_ref[...] + ob_ref[...]
    heads = []
    for h in range(GLA_HEADS):
        oh = o[:, h * GLA_DV:(h + 1) * GLA_DV]
        heads.append(oh * lax.rsqrt(jnp.mean(oh * oh, axis=-1, keepdims=True) + EPS))
    r = r_ref[...].astype(f32)
    yb = jnp.concatenate(heads, axis=1) * gn_ref[...] * (r * jax.nn.sigmoid(r))

    merged = (jax.nn.sigmoid(ga_ref[...].astype(f32)) * jnp.dot(ya.astype(bf16), ws5_ref[...], preferred_element_type=f32)
              + jax.nn.sigmoid(gb_ref[...].astype(f32)) * jnp.dot(yb.astype(bf16), wgla_ref[...], preferred_element_type=f32))
    x1 = x_ref[...] + mod_ref[0, 2:3, :] * jnp.dot(merged.astype(bf16), wout_ref[...], preferred_element_type=f32)
    x1_ref[...] = x1

    y2 = x1 * lax.rsqrt(jnp.mean(x1 * x1, axis=-1, keepdims=True) + EPS) * n2_ref[...]
    h2 = y2 * (1.0 + mod_ref[0, 4:5, :]) + mod_ref[0, 3:4, :]
    h2_ref[...] = h2

    logits = jnp.dot(h2, wr_ref[...], precision=HIGHEST, preferred_element_type=f32) + br_ref[...]
    lane = lax.broadcasted_iota(i32, logits.shape, 1)
    lane_f = lane.astype(f32)
    vals = logits
    top_v, top_i = [], []
    for _ in range(TOP_K):
        m = jnp.max(vals, axis=-1, keepdims=True)
        idx = jnp.min(jnp.where(vals == m, lane_f, float(LANES)), axis=-1, keepdims=True).astype(i32)
        top_v.append(m)
        top_i.append(idx)
        vals = jnp.where(lane == idx, -jnp.inf, vals)
    ex = [jnp.exp(v - top_v[0]) for v in top_v]
    inv = 1.0 / (ex[0] + ex[1] + ex[2] + ex[3])
    gates = jnp.zeros(logits.shape, f32)
    idxs = jnp.zeros(logits.shape, i32)
    for kk in range(TOP_K):
        gates = jnp.where(lane == kk, ex[kk] * inv, gates)
        idxs = jnp.where(lane == kk, top_i[kk], idxs)
    gate_ref[...] = gates
    idx_ref[...] = idxs


def _post(y_s5, o_f, o_b, proj, x, mod_l, wglu, bglu, ws5, gnorm, wgla, wout, norm2, wr, br):
    const = lambda shape: pl.BlockSpec(shape, lambda i: (0,) * len(shape), pipeline_mode=pl.Buffered(1))
    r_blk = (S5_WIDTH + 2 * GLA_QK + GLA_V) // GLA_V
    ga_blk = 4096 // D_MODEL
    return pl.pallas_call(
        _post_kernel,
        out_shape=(jax.ShapeDtypeStruct((NT, D_MODEL), f32),
                   jax.ShapeDtypeStruct((NT, D_MODEL), f32),
                   jax.ShapeDtypeStruct((NT, LANES), f32),
                   jax.ShapeDtypeStruct((NT, LANES), i32)),
        grid=(NT // POST_TM,),
        in_specs=[
            pl.BlockSpec((POST_TM, S5_WIDTH), lambda i: (i, 0)),
            pl.BlockSpec((POST_TM, GLA_V), lambda i: (i, 0)),
            pl.BlockSpec((POST_TM, GLA_V), lambda i: (i, 0)),
            pl.BlockSpec((POST_TM, GLA_V), lambda i: (i, r_blk)),
            pl.BlockSpec((POST_TM, D_MODEL), lambda i: (i, ga_blk)),
            pl.BlockSpec((POST_TM, D_MODEL), lambda i: (i, ga_blk + 1)),
            pl.BlockSpec((POST_TM, D_MODEL), lambda i: (i, 0)),
            pl.BlockSpec((1, 6, D_MODEL), lambda i: (_row_seq_of_tile(i, POST_TM), 0, 0)),
            const((S5_WIDTH, S5_WIDTH)), const((1, S5_WIDTH)), const((S5_WIDTH, D_MODEL)),
            const((1, GLA_V)), const((GLA_V, D_MODEL)), const((D_MODEL, D_MODEL)), const((1, D_MODEL)),
            const((D_MODEL, LANES)), const((1, LANES)),
        ],
        out_specs=(pl.BlockSpec((POST_TM, D_MODEL), lambda i: (i, 0)),
                   pl.BlockSpec((POST_TM, D_MODEL), lambda i: (i, 0)),
                   pl.BlockSpec((POST_TM, LANES), lambda i: (i, 0)),
                   pl.BlockSpec((POST_TM, LANES), lambda i: (i, 0))),
        compiler_params=_cparams(("arbitrary",)),
        name="post",
    )(y_s5, o_f, o_b, proj, proj, proj, x, mod_l, wglu, bglu, ws5, gnorm, wgla, wout, norm2, wr, br)


def _routing(top_idx):
    m = NT * TOP_K
    e_flat = top_idx.reshape(-1)
    experts = jnp.arange(N_EXPERTS, dtype=i32)
    onehot = (e_flat[:, None] == experts[None, :]).astype(i32)
    csum = jnp.cumsum(onehot, axis=0)
    counts = csum[-1]
    starts = jnp.cumsum(counts) - counts
    nblk = (counts + MOE_BM - 1) // MOE_BM
    blk_end = jnp.cumsum(nblk)
    blk_start = blk_end - nblk
    dest = jnp.sum(onehot * (blk_start[None, :] * MOE_BM + csum - 1), axis=1).astype(i32)

    order = jnp.argsort(e_flat, stable=True).astype(i32)
    p = jnp.arange(MOE_ROWS, dtype=i32)
    e_p = jnp.minimum(jnp.sum((blk_end[None, :] <= (p // MOE_BM)[:, None]).astype(i32), axis=1), N_EXPERTS - 1)
    oh_p = (e_p[:, None] == experts[None, :]).astype(i32)
    pick = lambda v: jnp.sum(oh_p * v[None, :], axis=1)
    idx_in = p - pick(blk_start) * MOE_BM
    valid = (p < blk_end[-1] * MOE_BM) & (idx_in < pick(counts))
    src = order[jnp.clip(pick(starts) + idx_in, 0, m - 1)] // TOP_K
    row_tok = jnp.where(valid, src, 0).astype(i32)

    n_units = (nblk + MOE_RB - 1) // MOE_RB
    unit_end = jnp.cumsum(n_units)
    unit_start = unit_end - n_units
    total_units = unit_end[-1]
    u = jnp.arange(MOE_UNITS, dtype=i32)
    used = u < total_units
    e_of_u = jnp.minimum(jnp.sum((unit_end[None, :] <= u[:, None]).astype(i32), axis=1), N_EXPERTS - 1)
    local = u - unit_start[e_of_u]
    u_blk0 = blk_start[e_of_u] + local * MOE_RB
    u_nblk = jnp.minimum(MOE_RB, nblk[e_of_u] - local * MOE_RB)
    last_e = e_of_u[jnp.maximum(total_units - 1, 0)]
    u_exp = jnp.where(used, e_of_u, last_e).astype(i32)
    u_blk0 = jnp.where(used, u_blk0, 0).astype(i32)
    u_nblk = jnp.where(used, u_nblk, 0).astype(i32)
    return row_tok, dest, u_exp, u_blk0, u_nblk, blk_end[-1:].astype(i32)


def _gather_kernel(tok_ref, src_ref, o_ref, stage_ref, sem):
    i = pl.program_id(0)
    S = GATHER_SPLIT

    def row_copy(t, slot, r):
        return pltpu.make_async_copy(src_ref.at[pl.ds(pl.multiple_of(t * S, S), S), :],
                                     stage_ref.at[slot, pl.ds(pl.multiple_of(r * S, S), S), :], sem.at[slot])

    def issue(step, slot):
        base = step * GATHER_GB
        def body(p, c):
            row_copy(tok_ref[base + 2 * p], slot, 2 * p).start(priority=0)
            row_copy(tok_ref[base + 2 * p + 1], slot, 2 * p + 1).start(priority=1)
            return c

        lax.fori_loop(0, GATHER_GB // 2, body, 0, unroll=8)

    @pl.when(i == 0)
    def _():
        issue(0, 0)

    @pl.when(i + 1 < pl.num_programs(0))
    def _():
        issue(i + 1, (i + 1) % 2)

    def finish(slot):
        lax.fori_loop(0, GATHER_GB, lambda r, c: (row_copy(0, slot, r).wait(), c)[1], 0, unroll=32)
        parts = [stage_ref[slot, pl.ds(k, GATHER_GB, stride=S), :] for k in range(S)]
        o_ref[...] = jnp.concatenate(parts, axis=1).astype(bf16)

    for slot in range(2):
        pl.when(i % 2 == slot)(functools.partial(finish, slot))


def _gather_rows(h2, row_tok):
    S = GATHER_SPLIT
    return pl.pallas_call(
        _gather_kernel,
        out_shape=jax.ShapeDtypeStruct((MOE_ROWS, D_MODEL), bf16),
        grid_spec=pltpu.PrefetchScalarGridSpec(
            num_scalar_prefetch=1,
            grid=(MOE_ROWS // GATHER_GB,),
            in_specs=[pl.BlockSpec(memory_space=pl.ANY)],
            out_specs=pl.BlockSpec((GATHER_GB, D_MODEL), lambda i, tok: (i, 0)),
            scratch_shapes=[pltpu.VMEM((2, GATHER_GB * S, D_MODEL // S), f32), pltpu.SemaphoreType.DMA((2,))],
        ),
        compiler_params=_cparams(("arbitrary",)),
        name="moe_gather",
    )(row_tok, h2.reshape(NT * S, D_MODEL // S))


def _deinterleave_matrix():
    half = MOE_TA // 2
    src = jnp.arange(MOE_TA)[:, None]
    dst = jnp.arange(MOE_TA)[None, :]
    return (dst == (src % 2) * half + src // 2).astype(bf16)


def _moe_kernel(uexp_ref, ublk_ref, unb_ref, nused_ref, x_hbm, w1_ref, b1_ref, perm_ref, w2_ref, b2_ref,
                y_hbm, x_buf, act_buf, w_buf, y_buf, zero_buf, sem_in, sem_out):
    u = pl.program_id(0)
    j = pl.program_id(1)
    nb = unb_ref[u]
    row0 = ublk_ref[u] * MOE_BM
    big = MOE_QB * MOE_BM
    n_big = nb // MOE_QB
    n_small = nb - n_big * MOE_QB
    small0 = n_big * big

    def in_copy(r):
        rows = pl.ds(pl.multiple_of(r * MOE_BM, MOE_BM), MOE_BM)
        src = x_hbm.at[pl.ds(pl.multiple_of(row0 + r * MOE_BM, MOE_BM), MOE_BM), :]
        return pltpu.make_async_copy(src, x_buf.at[rows, :], sem_in)

    @pl.when((j == 0) & (nb > 0))
    def _():
        lax.fori_loop(0, nb, lambda r, c: (in_copy(r).start(), c)[1], 0)
        lax.fori_loop(0, nb, lambda r, c: (in_copy(r).wait(), c)[1], 0)

    @pl.when((j < MOE_J1) & (nb > 0))
    def _():
        w_buf[...] = w1_ref[0, 0].astype(bf16)
        half = MOE_TA // 2

        def act_rows(start, m):
            rows = pl.ds(pl.multiple_of(start, MOE_BM), m)
            h = jnp.dot(x_buf[rows, :], w_buf[...], preferred_element_type=f32) + b1_ref[0, 0]
            hb = h.astype(bf16)
            parts = []
            for s in range(MOE_TN // MOE_TA):
                hp = jnp.dot(hb[:, s * MOE_TA:(s + 1) * MOE_TA], perm_ref[...], preferred_element_type=f32)
                hg = jnp.minimum(hp[:, :half], SWIGLU_LIMIT)
                hl = jnp.clip(hp[:, half:], -SWIGLU_LIMIT, SWIGLU_LIMIT)
                parts.append(hg * jax.nn.sigmoid(SWIGLU_ALPHA * hg) * (hl + 1.0))
            act_buf[j, rows, :] = jnp.concatenate(parts, axis=1).astype(bf16)

        lax.fori_loop(0, n_big, lambda q, c: (act_rows(q * big, big), c)[1], 0)
        lax.fori_loop(0, n_small, lambda r, c: (act_rows(small0 + r * MOE_BM, MOE_BM), c)[1], 0)

    @pl.when((j >= MOE_J1) & (nb > 0))
    def _():
        w_buf[...] = w2_ref[0, 0].astype(bf16)
        col0 = pl.multiple_of((j - MOE_J1) * MOE_TN, MOE_TN)

        def out_copy(start, m):
            rows = pl.ds(pl.multiple_of(start, MOE_BM), m)
            dst = y_hbm.at[pl.ds(pl.multiple_of(row0 + start, MOE_BM), m), pl.ds(col0, MOE_TN)]
            return pltpu.make_async_copy(y_buf.at[rows, :], dst, sem_out)

        def out_rows(start, m):
            rows = pl.ds(pl.multiple_of(start, MOE_BM), m)
            acc = b2_ref[0, 0] + jnp.zeros((m, MOE_TN), f32)
            for jj in range(MOE_J1):
                acc = acc + jnp.dot(act_buf[jj, rows, :], w_buf[jj * MOE_TA:(jj + 1) * MOE_TA, :],
                                    preferred_element_type=f32)
            y_buf[rows, :] = acc
            out_copy(start, m).start()

        lax.fori_loop(0, n_big, lambda q, c: (out_rows(q * big, big), c)[1], 0)
        lax.fori_loop(0, n_small, lambda r, c: (out_rows(small0 + r * MOE_BM, MOE_BM), c)[1], 0)
        lax.fori_loop(0, n_big, lambda q, c: (out_copy(q * big, big).wait(), c)[1], 0)
        lax.fori_loop(0, n_small, lambda r, c: (out_copy(small0 + r * MOE_BM, MOE_BM).wait(), c)[1], 0)

    @pl.when((u == MOE_UNITS - 1) & (j == MOE_J1 + MOE_J2 - 1))
    def _():
        zero_buf[...] = jnp.zeros(zero_buf.shape, f32)

        def fill_copy(r):
            dst = y_hbm.at[pl.ds(pl.multiple_of(r * MOE_BM, MOE_BM), MOE_BM), :]
            return pltpu.make_async_copy(zero_buf, dst, sem_out)

        lax.fori_loop(nused_ref[0], MOE_BLOCKS, lambda r, c: (fill_copy(r).start(), c)[1], 0)
        lax.fori_loop(nused_ref[0], MOE_BLOCKS, lambda r, c: (fill_copy(r).wait(), c)[1], 0)


def _moe_experts(x_sorted, u_exp, u_blk0, u_nblk, n_used, w1, b1, w2, b2, layer):
    def w1_idx(u, j, uexp, ublk, unb, nused):
        return (layer, uexp[u], 0, jnp.where(unb[u] > 0, jnp.minimum(j, MOE_J1 - 1), MOE_J1 - 1))

    def w2_idx(u, j, uexp, ublk, unb, nused):
        return (layer, uexp[u], 0, jnp.where(unb[u] > 0, jnp.maximum(j - MOE_J1, 0), MOE_J2 - 1))

    return pl.pallas_call(
        _moe_kernel,
        out_shape=jax.ShapeDtypeStruct((MOE_ROWS, D_MODEL), f32),
        grid_spec=pltpu.PrefetchScalarGridSpec(
            num_scalar_prefetch=4,
            grid=(MOE_UNITS, MOE_J1 + MOE_J2),
            in_specs=[
                pl.BlockSpec(memory_space=pl.ANY),
                pl.BlockSpec((1, 1, D_MODEL, MOE_TN), w1_idx),
                pl.BlockSpec((1, 1, 1, MOE_TN), w1_idx),
                pl.BlockSpec((MOE_TA, MOE_TA), lambda u, j, *_: (0, 0)),
                pl.BlockSpec((1, 1, D_MODEL, MOE_TN), w2_idx),
                pl.BlockSpec((1, 1, 1, MOE_TN), w2_idx),
            ],
            out_specs=pl.BlockSpec(memory_space=pl.ANY),
            scratch_shapes=[
                pltpu.VMEM((MOE_RMAX, D_MODEL), bf16),
                pltpu.VMEM((MOE_J1, MOE_RMAX, MOE_TA), bf16),
                pltpu.VMEM((D_MODEL, MOE_TN), bf16),
                pltpu.VMEM((MOE_RMAX, MOE_TN), f32),
                pltpu.VMEM((MOE_BM, D_MODEL), f32),
                pltpu.SemaphoreType.DMA(()),
                pltpu.SemaphoreType.DMA(()),
            ],
        ),
        compiler_params=_cparams(("arbitrary", "arbitrary")),
        name="moe_experts",
    )(u_exp, u_blk0, u_nblk, n_used, x_sorted, w1, b1, _deinterleave_matrix(), w2, b2)


def _combine_kernel(dest_ref, y_hbm, gate_ref, x1_ref, mod_ref, nf_ref, o_ref, y_buf, sem, *, final_norm):
    i = pl.program_id(0)

    def row_copy(p, slot, kk, r):
        return pltpu.make_async_copy(y_hbm.at[pl.ds(p, 1), :], y_buf.at[slot, kk, pl.ds(r, 1), :], sem.at[slot])

    def issue(step, slot):
        base = step * COMB_TM * TOP_K

        def body(r, c):
            for kk in range(TOP_K):
                row_copy(dest_ref[base + r * TOP_K + kk], slot, kk, r).start(priority=kk % 2)
            return c

        lax.fori_loop(0, COMB_TM, body, 0, unroll=4)

    @pl.when(i == 0)
    def _():
        issue(0, 0)

    @pl.when(i + 1 < pl.num_programs(0))
    def _():
        issue(i + 1, (i + 1) % 2)

    def finish(slot):
        def wait(r, c):
            for kk in range(TOP_K):
                row_copy(0, slot, kk, r).wait()
            return c

        lax.fori_loop(0, COMB_TM, wait, 0, unroll=8)
        gates = gate_ref[...]
        acc = jnp.zeros((COMB_TM, D_MODEL), f32)
        for kk in range(TOP_K):
            acc = acc + gates[:, kk:kk + 1] * y_buf[slot, kk]
        x2 = x1_ref[...] + mod_ref[0, 5:6, :] * acc
        if final_norm:
            x2 = x2 * lax.rsqrt(jnp.mean(x2 * x2, axis=-1, keepdims=True) + EPS) * nf_ref[...]
        o_ref[...] = x2

    for slot in range(2):
        pl.when(i % 2 == slot)(functools.partial(finish, slot))


def _combine(y_sorted, dest, gates, x1, mod_l, norm_f, final_norm):
    return pl.pallas_call(
        functools.partial(_combine_kernel, final_norm=final_norm),
        out_shape=jax.ShapeDtypeStruct((NT, D_MODEL), f32),
        grid_spec=pltpu.PrefetchScalarGridSpec(
            num_scalar_prefetch=1,
            grid=(NT // COMB_TM,),
            in_specs=[
                pl.BlockSpec(memory_space=pl.ANY),
                pl.BlockSpec((COMB_TM, LANES), lambda i, d: (i, 0)),
                pl.BlockSpec((COMB_TM, D_MODEL), lambda i, d: (i, 0)),
                pl.BlockSpec((1, 6, D_MODEL), lambda i, d: (_row_seq_of_tile(i, COMB_TM), 0, 0)),
                pl.BlockSpec((1, D_MODEL), lambda i, d: (0, 0)),
            ],
            out_specs=pl.BlockSpec((COMB_TM, D_MODEL), lambda i, d: (i, 0)),
            scratch_shapes=[pltpu.VMEM((2, TOP_K, COMB_TM, D_MODEL), f32), pltpu.SemaphoreType.DMA((2,))],
        ),
        compiler_params=_cparams(("arbitrary",)),
        name="moe_combine_final" if final_norm else "moe_combine",
    )(dest, y_sorted, gates, x1, mod_l, norm_f)


def kernel(x_prompt, x_sample, c, state_s5_re, state_s5_im, state_gla, c_ctx, w_mod, b_mod, norm1, w_in, s5_lam_re, s5_lam_im, s5_log_dt, s5_b_re, s5_b_im, s5_c_re, s5_c_im, s5_d, s5_w_glu, s5_b_glu, w_s5_out, gla_w_g2, gla_b_g, gla_norm, w_gla_out, w_out, norm2, w_router, b_router, w_e1, b_e1, w_e2, b_e2, norm_f):
    cond = jnp.concatenate([c_ctx[None], c, jnp.zeros((N_MODROWS - 1 - DEC_BATCH, D_MODEL), f32)], axis=0)
    mod = _modulation(cond, w_mod, b_mod).reshape(DEPTH, N_MODROWS, 6, D_MODEL)
    x = _embed(x_prompt, x_sample, _grid_pos_embed())

    glr0 = S5_WIDTH + 2 * GLA_QK + 2 * GLA_V
    glr1 = glr0 + N_DIRS * GLA_RANK
    s5_re_out, s5_im_out, gla_out = [], [], []
    toeplitz, w_state, w_carry, coef = jax.vmap(_s5_weights)(
        s5_lam_re, s5_lam_im, s5_log_dt, s5_b_re, s5_b_im, s5_c_re, s5_c_im, s5_d)
    b_e1_r = b_e1.reshape(DEPTH, N_EXPERTS, 1, 2 * D_MODEL)
    b_e2_r = b_e2.reshape(DEPTH, N_EXPERTS, 1, D_MODEL)
    for l in range(DEPTH):
        mod_l = mod[l]
        w_main = jnp.concatenate([w_in[l, :, :glr0], w_in[l, :, glr1:]], axis=1).astype(bf16)
        w_glr = w_in[l, :, glr0:glr1].astype(bf16)
        zg = jnp.zeros((GLA_RANK, GLA_QK), f32)
        w_g2bd = jnp.concatenate([jnp.concatenate([gla_w_g2[l, 0], zg], axis=1),
                                  jnp.concatenate([zg, gla_w_g2[l, 1]], axis=1)], axis=0)
        b_g = gla_b_g[l].reshape(1, N_DIRS * GLA_QK)
        h0 = _s5_init_rows(state_s5_re[:, l], state_s5_im[:, l])
        w_r = jnp.concatenate([w_router[l], jnp.zeros((D_MODEL, LANES - N_EXPERTS), f32)], axis=1)
        b_r = jnp.concatenate([b_router[l], jnp.full((LANES - N_EXPERTS,), -jnp.inf, f32)]).reshape(1, LANES)

        proj, logg = _pre(x, mod_l, norm1[l].reshape(1, D_MODEL), w_main, w_glr, w_g2bd, b_g)
        y_g, s5_fin = _s5_scan(_s5_to_groups(proj[:, :S5_WIDTH]), toeplitz[l], w_state[l], w_carry[l], coef[l], h0)
        y_s5 = _s5_from_groups(y_g)
        o_f, st_f = _gla_direction(proj, logg, state_gla, l, False)
        o_b, st_b = _gla_direction(proj, logg, state_gla, l, True)

        x1, h2, gates, top_idx = _post(
            y_s5, o_f, o_b, proj, x, mod_l,
            s5_w_glu[l].astype(bf16), s5_b_glu[l].reshape(1, S5_WIDTH), w_s5_out[l].astype(bf16),
            jnp.tile(gla_norm[l], GLA_HEADS).reshape(1, GLA_V), w_gla_out[l].astype(bf16), w_out[l].astype(bf16),
            norm2[l].reshape(1, D_MODEL), w_r, b_r)

        row_tok, dest, u_exp, u_blk0, u_nblk, n_used = _routing(top_idx[:, :TOP_K])
        x_sorted = _gather_rows(h2, row_tok)
        y_sorted = _moe_experts(x_sorted, u_exp, u_blk0, u_nblk, n_used, w_e1, b_e1_r, w_e2, b_e2_r, l)
        x = _combine(y_sorted, dest, gates, x1, mod_l, norm_f.reshape(1, D_MODEL), l == DEPTH - 1)

        P = S5_STATE
        fin = jnp.transpose(s5_fin, (1, 0, 2))
        s5_re_out.append(jnp.stack([fin[..., 0:P], fin[..., 2 * P:3 * P]], axis=1))
        s5_im_out.append(jnp.stack([fin[..., P:2 * P], fin[..., 3 * P:4 * P]], axis=1))
        gla_out.append(jnp.stack([st_f, st_b], axis=1))

    y_prompt = x[:N_CTX].reshape(BATCH, SEQ, D_MODEL)
    y_sample = x[N_CTX:].reshape(DEC_BATCH, DEC_SEQ, D_MODEL)
    return (y_prompt, y_sample, jnp.stack(s5_re_out, axis=1), jnp.stack(s5_im_out, axis=1),
            jnp.stack(gla_out, axis=1))
```
